```python
import jax, jax.numpy as jnp
from jax import lax
import numpy as np

D_MODEL = 1024
BATCH = 1
SEQ = 16384
DEPTH = 2
DEC_BATCH = 32
DEC_SEQ = 8
PAST_LEN = 16384
PAGE_SIZE = 128

HEAD_DIM = 64
RW_HEADS = 8
FOX_HEADS = 8
MOBA_HEADS = 8
RW_WIDTH = RW_HEADS * HEAD_DIM
FOX_WIDTH = FOX_HEADS * HEAD_DIM
MOBA_WIDTH = MOBA_HEADS * HEAD_DIM
W_LORA = 64
A_LORA = 64
G_LORA = 128
RW_COLS = 3 * RW_WIDTH + W_LORA + A_LORA + G_LORA
FOX_COLS = 3 * FOX_WIDTH + FOX_HEADS
MOBA_COLS = 3 * MOBA_WIDTH
N_BRANCH = 3
GATE_COLS = N_BRANCH * D_MODEL
IN_COLS = RW_COLS + FOX_COLS + MOBA_COLS + GATE_COLS
D_FF = 2816
MOBA_BLOCK = 256
MOBA_TOPK = 3
Q_BLOCK = 128
RMS_EPS = 1e-6
GN_EPS = 64e-5
FORGET_BIAS = 2.0
ATTN_SCALE = HEAD_DIM ** -0.5
NEG = -1e30

kernel_name = "hybrid_rwkv7_fox_moba_decoder_step"


def rms_norm(x, g):
    xf = x.astype(jnp.float32)
    y = xf * lax.rsqrt(jnp.mean(xf * xf, axis=-1, keepdims=True) + RMS_EPS)
    return (y * g.astype(jnp.float32)).astype(x.dtype)


def swiglu(x, wg, wu, wd):
    return (jax.nn.silu(x @ wg) * (x @ wu)) @ wd


def alibi_slopes(n):
    return 2.0 ** (-8.0 * jnp.arange(1, n + 1, dtype=jnp.float32) / n)


def _offsets(sizes):
    out, acc = [], 0
    for s in sizes[:-1]:
        acc += s
        out.append(acc)
    return out


def gather_pages(pool, l, page_table):
    g = pool[l, page_table]
    return g.reshape((g.shape[0], g.shape[1] * g.shape[2]) + g.shape[3:])


def _wkv_step(S, inp):
    r, w, k, v, a, b = inp
    sa = jnp.einsum('bhij,bhj->bhi', S, a)
    S = S * w[:, :, None, :] + sa[..., None] * b[:, :, None, :] + v[..., None] * k[:, :, None, :]
    y = jnp.einsum('bhij,bhj->bhi', S, r)
    return S, y


def rwkv_time_mix(xm, wkv0, w0, w_up, a0, a_up, g_up, k_k, k_a, r_k, ln_w, ln_b):
    f32 = jnp.float32
    B, T, _ = xm.shape
    r, k, v, wd, ad, gd = jnp.split(xm, _offsets([RW_WIDTH, RW_WIDTH, RW_WIDTH, W_LORA, A_LORA, G_LORA]), axis=-1)
    heads = lambda t: t.reshape(B, T, RW_HEADS, HEAD_DIM)
    logw = -jax.nn.softplus(-(w0 + jnp.tanh(wd) @ w_up).astype(f32)) - 0.5
    decay = jnp.exp(-jnp.exp(logw))
    a = heads(jax.nn.sigmoid((a0 + ad @ a_up).astype(f32)))
    g = jax.nn.sigmoid(gd) @ g_up
    kk = heads((k * k_k).astype(f32))
    kk = kk / jnp.maximum(jnp.linalg.norm(kk, axis=-1, keepdims=True), 1e-12)
    k_h = heads(k.astype(f32)) * (1.0 + (a - 1.0) * k_a.reshape(RW_HEADS, HEAD_DIM).astype(f32))
    r_h = heads(r.astype(f32))
    v_h = heads(v.astype(f32))
    xs = tuple(jnp.moveaxis(t, 1, 0) for t in (r_h, heads(decay), k_h, v_h, -kk, kk * a))
    S_T, y = lax.scan(_wkv_step, wkv0.astype(f32), xs)
    y = jnp.moveaxis(y, 0, 1)
    mu = jnp.mean(y, axis=-1, keepdims=True)
    var = jnp.mean(jnp.square(y - mu), axis=-1, keepdims=True)
    yn = (y - mu) * lax.rsqrt(var + GN_EPS) * ln_w.reshape(RW_HEADS, HEAD_DIM) + ln_b.reshape(RW_HEADS, HEAD_DIM)
    bonus = jnp.sum(r_h * k_h * r_k.astype(f32), axis=-1, keepdims=True) * v_h
    o = ((yn + bonus).reshape(B, T, RW_WIDTH) * g.astype(f32)).astype(xm.dtype)
    return o, S_T


def fox_attend(q, qpos, cq, k, v, kpos, ck):
    s = jnp.einsum('bqhd,bkhd->bhqk', q, k).astype(jnp.float32) * ATTN_SCALE
    s = s + jnp.swapaxes(cq, 1, 2)[..., :, None] - jnp.swapaxes(ck, 1, 2)[..., None, :]
    s = jnp.where(kpos[None, :] <= qpos[:, None], s, -jnp.inf)
    p = jax.nn.softmax(s, axis=-1)
    return jnp.einsum('bhqk,bkhd->bqhd', p.astype(v.dtype), v)


def fox_prompt(q, k, v, logf):
    B, T, H, d = q.shape
    c = jnp.cumsum(logf, axis=1)
    pos = jnp.arange(T)
    nq = T // Q_BLOCK
    qb = q.reshape(B, nq, Q_BLOCK, H, d).swapaxes(0, 1)
    cb = c.reshape(B, nq, Q_BLOCK, H).swapaxes(0, 1)
    pb = pos.reshape(nq, Q_BLOCK)
    out = lax.map(lambda a: fox_attend(a[0], a[1], a[2], k, v, pos, c), (qb, pb, cb))
    return out.swapaxes(0, 1).reshape(B, T, H, d)


def fox_sample(q, k, v, logf, k_past, v_past, logf_past):
    P, T = k_past.shape[1], q.shape[1]
    k_all = jnp.concatenate([k_past, k], axis=1)
    v_all = jnp.concatenate([v_past, v], axis=1)
    c = jnp.cumsum(jnp.concatenate([logf_past.astype(jnp.float32), logf], axis=1), axis=1)
    qpos = P + jnp.arange(T)
    return fox_attend(q, qpos, c[:, P:], k_all, v_all, jnp.arange(P + T), c)


def to_blocks(t):
    B, T, H, d = t.shape
    nb = -(-T // MOBA_BLOCK)
    t = jnp.pad(t, ((0, 0), (0, nb * MOBA_BLOCK - T), (0, 0), (0, 0)))
    return t.reshape(B, nb, MOBA_BLOCK, H, d)


def moba_attend(q, qpos, kb, vb, kmean, slopes):
    B, Tq, H, d = q.shape
    nb = kb.shape[1]
    n_sel = min(MOBA_TOPK, nb)
    own = qpos // MOBA_BLOCK
    bs = jnp.einsum('bqhd,bnhd->bhqn', q, kmean).astype(jnp.float32)
    bs = jnp.where(jnp.arange(nb)[None, :] < own[:, None], bs, NEG)
    _, sel = lax.top_k(bs, n_sel)
    sel_ok = jnp.arange(n_sel)[None, :] < own[:, None]
    idx = jnp.concatenate([sel, jnp.broadcast_to(own[None, None, :, None].astype(sel.dtype), (B, H, Tq, 1))], axis=-1)
    ok = jnp.concatenate([sel_ok, jnp.ones((Tq, 1), dtype=bool)], axis=-1)
    bi = jnp.arange(B)[:, None, None, None]
    hi = jnp.arange(H)[None, :, None, None]
    kg = kb[bi, idx, :, hi]
    vg = vb[bi, idx, :, hi]
    s = jnp.einsum('bqhd,bhqnkd->bhqnk', q, kg).astype(jnp.float32) * ATTN_SCALE
    kpos = idx[..., None] * MOBA_BLOCK + jnp.arange(MOBA_BLOCK)
    dist = qpos[None, None, :, None, None] - kpos
    s = s - slopes[None, :, None, None, None] * dist.astype(jnp.float32)
    mask = ok[None, None, :, :, None] & (dist >= 0)
    s = jnp.where(mask, s, -jnp.inf)
    p = jax.nn.softmax(s.reshape(B, H, Tq, -1), axis=-1).reshape(s.shape)
    return jnp.einsum('bhqnk,bhqnkd->bqhd', p.astype(vg.dtype), vg)


def moba_prompt(q, k, v, slopes):
    B, T, H, d = q.shape
    kb, vb = to_blocks(k), to_blocks(v)
    kmean = jnp.mean(kb.astype(jnp.float32), axis=2).astype(q.dtype)
    nq = T // Q_BLOCK
    qb = q.reshape(B, nq, Q_BLOCK, H, d).swapaxes(0, 1)
    pb = jnp.arange(T).reshape(nq, Q_BLOCK)
    out = lax.map(lambda a: moba_attend(a[0], a[1], kb, vb, kmean, slopes), (qb, pb))
    return out.swapaxes(0, 1).reshape(B, T, H, d)


def moba_sample(q, k, v, k_past, v_past, slopes):
    P, T = k_past.shape[1], q.shape[1]
    kb = to_blocks(jnp.concatenate([k_past, k], axis=1))
    vb = to_blocks(jnp.concatenate([v_past, v], axis=1))
    kmean = jnp.mean(kb.astype(jnp.float32), axis=2).astype(q.dtype)
    return moba_attend(q, P + jnp.arange(T), kb, vb, kmean, slopes)


def run_layer(x, prev_u, wkv0, past, W, slopes):
    B, T, _ = x.shape
    nrm = W['norms']
    h = x + 0.5 * rms_norm(swiglu(rms_norm(x, nrm[0]), W['ffn_gate'][0], W['ffn_up'][0], W['ffn_down'][0]), nrm[1])
    u = rms_norm(h, nrm[2])
    w_in = W['w_in']
    proj = u @ w_in
    rw, fq, fk, fv, fl, mq, mk, mv, gt = jnp.split(proj, _offsets(
        [RW_COLS, FOX_WIDTH, FOX_WIDTH, FOX_WIDTH, FOX_HEADS, MOBA_WIDTH, MOBA_WIDTH, MOBA_WIDTH, GATE_COLS]), axis=-1)
    prev_rw = prev_u @ w_in[:, :RW_COLS]
    rw_prev = jnp.concatenate([prev_rw[:, None, :], rw[:, :-1]], axis=1)
    xm = rw + (rw_prev - rw) * W['rwkv_mu']
    o_rw, wkv_new = rwkv_time_mix(xm, wkv0, W['rwkv_w0'], W['rwkv_w_up'], W['rwkv_a0'], W['rwkv_a_up'],
                                  W['rwkv_g_up'], W['rwkv_k_k'], W['rwkv_k_a'], W['rwkv_r_k'],
                                  W['rwkv_ln_w'], W['rwkv_ln_b'])
    heads = lambda t, H: t.reshape(B, T, H, HEAD_DIM)
    fq, fk, fv = heads(fq, FOX_HEADS), heads(fk, FOX_HEADS), heads(fv, FOX_HEADS)
    logf = jax.nn.log_sigmoid((fl + W['b_forget']).astype(jnp.float32))
    mq, mk, mv = heads(mq, MOBA_HEADS), heads(mk, MOBA_HEADS), heads(mv, MOBA_HEADS)
    if past is None:
        o_fox = fox_prompt(fq, fk, fv, logf)
        o_moba = moba_prompt(mq, mk, mv, slopes)
    else:
        c_fk, c_fv, c_fl, c_mk, c_mv, page_table, l = past
        o_fox = fox_sample(fq, fk, fv, logf, gather_pages(c_fk, l, page_table),
                           gather_pages(c_fv, l, page_table), gather_pages(c_fl, l, page_table))
        o_moba = moba_sample(mq, mk, mv, gather_pages(c_mk, l, page_table),
                             gather_pages(c_mv, l, page_table), slopes)
    gates = jax.nn.sigmoid(gt.astype(jnp.float32)).astype(x.dtype).reshape(B, T, N_BRANCH, D_MODEL)
    merged = (gates[:, :, 0] * (o_rw @ W['w_proj_rwkv'])
              + gates[:, :, 1] * (o_fox.reshape(B, T, FOX_WIDTH) @ W['w_proj_fox'])
              + gates[:, :, 2] * (o_moba.reshape(B, T, MOBA_WIDTH) @ W['w_proj_moba']))
    h = h + rms_norm(merged @ W['w_out'], nrm[3])
    y = h + 0.5 * rms_norm(swiglu(rms_norm(h, nrm[4]), W['ffn_gate'][1], W['ffn_up'][1], W['ffn_down'][1]), nrm[5])
    return y, fk, fv, logf, mk, mv, wkv_new, u[:, -1]


def setup_inputs(seed: int = 0) -> dict:
    key = jax.random.key(seed)
    ks = jax.random.split(key, 32)
    f32 = jnp.float32

    def nrm(i, shape, scale):
        return scale * jax.random.normal(ks[i], shape, f32)

    n_pages = PAST_LEN // PAGE_SIZE
    n_used = DEC_BATCH * n_pages
    n_pool = n_used + n_used // 4
    page_table = jax.random.permutation(ks[0], n_pool)[:n_used].reshape(DEC_BATCH, n_pages).astype(jnp.int32)
    fox_kv = (DEPTH, n_pool, PAGE_SIZE, FOX_HEADS, HEAD_DIM)
    moba_kv = (DEPTH, n_pool, PAGE_SIZE, MOBA_HEADS, HEAD_DIM)
    return {
        'x_prompt': nrm(1, (BATCH, SEQ, D_MODEL), 1.0),
        'x_sample': nrm(2, (DEC_BATCH, DEC_SEQ, D_MODEL), 1.0),
        'cache_fox_k': nrm(3, fox_kv, 1.0),
        'cache_fox_v': nrm(4, fox_kv, 1.0),
        'cache_fox_logf': jax.nn.log_sigmoid(FORGET_BIAS + nrm(5, fox_kv[:4], 1.0)),
        'cache_moba_k': nrm(6, moba_kv, 1.0),
        'cache_moba_v': nrm(7, moba_kv, 1.0),
        'state_rwkv_wkv': nrm(8, (DEPTH, DEC_BATCH, RW_HEADS, HEAD_DIM, HEAD_DIM), 0.3),
        'state_rwkv_shift': nrm(9, (DEPTH, DEC_BATCH, D_MODEL), 1.0),
        'page_table': page_table,
        'norms': 1.0 + nrm(10, (DEPTH, 6, D_MODEL), 0.05),
        'w_in': nrm(11, (DEPTH, D_MODEL, IN_COLS), D_MODEL ** -0.5),
        'b_forget': FORGET_BIAS + nrm(12, (DEPTH, FOX_HEADS), 0.5),
        'rwkv_mu': jax.random.uniform(ks[13], (DEPTH, RW_COLS), f32),
        'rwkv_w0': -6.0 + 5.0 * jax.random.uniform(ks[14], (DEPTH, RW_WIDTH), f32),
        'rwkv_w_up': nrm(15, (DEPTH, W_LORA, RW_WIDTH), 0.5 * W_LORA ** -0.5),
        'rwkv_a0': nrm(16, (DEPTH, RW_WIDTH), 0.1),
        'rwkv_a_up': nrm(17, (DEPTH, A_LORA, RW_WIDTH), 0.5 * A_LORA ** -0.5),
        'rwkv_g_up': nrm(18, (DEPTH, G_LORA, RW_WIDTH), G_LORA ** -0.5),
        'rwkv_k_k': 0.85 + nrm(19, (DEPTH, RW_WIDTH), 0.05),
        'rwkv_k_a': 1.0 + nrm(20, (DEPTH, RW_WIDTH), 0.05),
        'rwkv_r_k': nrm(21, (DEPTH, RW_HEADS, HEAD_DIM), 0.1),
        'rwkv_ln_w': 1.0 + nrm(22, (DEPTH, RW_WIDTH), 0.05),
        'rwkv_ln_b': nrm(23, (DEPTH, RW_WIDTH), 0.02),
        'w_proj_rwkv': nrm(24, (DEPTH, RW_WIDTH, D_MODEL), RW_WIDTH ** -0.5),
        'w_proj_fox': nrm(25, (DEPTH, FOX_WIDTH, D_MODEL), FOX_WIDTH ** -0.5),
        'w_proj_moba': nrm(26, (DEPTH, MOBA_WIDTH, D_MODEL), MOBA_WIDTH ** -0.5),
        'w_out': nrm(27, (DEPTH, D_MODEL, D_MODEL), D_MODEL ** -0.5),
        'ffn_gate': nrm(28, (DEPTH, 2, D_MODEL, D_FF), D_MODEL ** -0.5),
        'ffn_up': nrm(29, (DEPTH, 2, D_MODEL, D_FF), D_MODEL ** -0.5),
        'ffn_down': nrm(30, (DEPTH, 2, D_FF, D_MODEL), D_FF ** -0.5),
    }


def reference(x_prompt, x_sample, cache_fox_k, cache_fox_v, cache_fox_logf, cache_moba_k, cache_moba_v,
              state_rwkv_wkv, state_rwkv_shift, page_table, norms, w_in, b_forget, rwkv_mu, rwkv_w0,
              rwkv_w_up, rwkv_a0, rwkv_a_up, rwkv_g_up, rwkv_k_k, rwkv_k_a, rwkv_r_k, rwkv_ln_w, rwkv_ln_b,
              w_proj_rwkv, w_proj_fox, w_proj_moba, w_out, ffn_gate, ffn_up, ffn_down):
    slopes = alibi_slopes(MOBA_HEADS)
    bp = x_prompt.shape[0]
    yp, ys = x_prompt, x_sample
    p_new, s_new = [], []
    for l in range(DEPTH):
        W = {'norms': norms[l], 'w_in': w_in[l], 'b_forget': b_forget[l], 'rwkv_mu': rwkv_mu[l],
             'rwkv_w0': rwkv_w0[l], 'rwkv_w_up': rwkv_w_up[l], 'rwkv_a0': rwkv_a0[l], 'rwkv_a_up': rwkv_a_up[l],
             'rwkv_g_up': rwkv_g_up[l], 'rwkv_k_k': rwkv_k_k[l], 'rwkv_k_a': rwkv_k_a[l], 'rwkv_r_k': rwkv_r_k[l],
             'rwkv_ln_w': rwkv_ln_w[l], 'rwkv_ln_b': rwkv_ln_b[l], 'w_proj_rwkv': w_proj_rwkv[l],
             'w_proj_fox': w_proj_fox[l], 'w_proj_moba': w_proj_moba[l], 'w_out': w_out[l],
             'ffn_gate': ffn_gate[l], 'ffn_up': ffn_up[l], 'ffn_down': ffn_down[l]}
        outp = run_layer(yp, jnp.zeros((bp, D_MODEL), x_prompt.dtype),
                         jnp.zeros((bp, RW_HEADS, HEAD_DIM, HEAD_DIM), jnp.float32), None, W, slopes)
        yp = outp[0]
        p_new.append(outp[1:])
        past = (cache_fox_k, cache_fox_v, cache_fox_logf, cache_moba_k, cache_moba_v, page_table, l)
        outs = run_layer(ys, state_rwkv_shift[l], state_rwkv_wkv[l], past, W, slopes)
        ys = outs[0]
        s_new.append(outs[1:])
    st = lambda rows, i: jnp.stack([r[i] for r in rows])
    return (yp, ys,
            st(p_new, 0), st(p_new, 1), st(p_new, 2), st(p_new, 3), st(p_new, 4), st(p_new, 5), st(p_new, 6),
            st(s_new, 0), st(s_new, 1), st(s_new, 2), st(s_new, 3), st(s_new, 4), st(s_new, 5), st(s_new, 6))
```

```python
import functools

import jax
import jax.numpy as jnp
from jax import lax
from jax.experimental import pallas as pl
from jax.experimental.pallas import tpu as pltpu

F32 = jnp.float32
BF16 = jnp.bfloat16

D_MODEL = 1024
HEAD_DIM = 64
N_HEADS = 8
WIDTH = N_HEADS * HEAD_DIM
W_LORA, A_LORA, G_LORA = 64, 64, 128
D_FF = 2816
PAGE = 128
MOBA_BLOCK = 256
MOBA_TOPK = 3
RMS_EPS = 1e-6
GN_EPS = 64e-5
ATTN_SCALE = HEAD_DIM ** -0.5
NEG = -1e30
LANES = 128
RW_PACK = 3 * WIDTH + 3 * LANES
IN_PACK = RW_PACK + 6 * WIDTH + 3 * D_MODEL + LANES
ROW_TILE = 256
RW_CHUNK = 128
ATT_TILE = 256
PAGES_PER_STEP = 8
VMEM_LIMIT = 56 * 1024 * 1024


def _cparams(*sem):
    return pltpu.CompilerParams(dimension_semantics=sem, vmem_limit_bytes=VMEM_LIMIT)


def _dot(a, b):
    return jnp.dot(a.astype(BF16), b.astype(BF16), preferred_element_type=F32)


def _dot_nt(a, b):
    return lax.dot_general(a.astype(BF16), b.astype(BF16), (((1,), (1,)), ((), ())), preferred_element_type=F32)


def _dot_tn(a, b):
    return lax.dot_general(a.astype(BF16), b.astype(BF16), (((0,), (0,)), ((), ())), preferred_element_type=F32)


def _split3(a):
    a1 = a.astype(BF16)
    r1 = a - a1.astype(F32)
    a2 = r1.astype(BF16)
    a3 = (r1 - a2.astype(F32)).astype(BF16)
    return a1, a2, a3


def _dot_exact_rhs(a, b01):
    a1, a2, a3 = _split3(a)
    return _dot(a1, b01) + _dot(a2, b01) + _dot(a3, b01)


def _dot_exact_lhs(a01, b):
    b1, b2, b3 = _split3(b)
    return _dot(a01, b1) + _dot(a01, b2) + _dot(a01, b3)


def _dot_nt_hi(a, b):
    a1, a2, a3 = _split3(a)
    b1, b2, b3 = _split3(b)
    return (_dot_nt(a1, b1) + _dot_nt(a1, b2) + _dot_nt(a2, b1)
            + _dot_nt(a1, b3) + _dot_nt(a2, b2) + _dot_nt(a3, b1))


def _rms(x, g):
    return x * lax.rsqrt(jnp.mean(x * x, axis=-1, keepdims=True) + RMS_EPS) * g


def _sigmoid(x):
    return 1.0 / (1.0 + jnp.exp(-x))


def _softplus(x):
    return jnp.maximum(x, 0.0) + jnp.log1p(jnp.exp(-jnp.abs(x)))


def _const_spec(shape):
    nd = len(shape)
    return pl.BlockSpec(shape, lambda *_: (0,) * nd, pipeline_mode=pl.Buffered(1))


def _ffn_kernel(x_ref, gpre_ref, gpost_ref, wg_ref, wu_ref, wd_ref, o_ref):
    x = x_ref[...]
    xn = _rms(x, gpre_ref[...]).astype(BF16)
    g = jnp.dot(xn, wg_ref[...], preferred_element_type=F32)
    u = jnp.dot(xn, wu_ref[...], preferred_element_type=F32)
    a = (g * _sigmoid(g) * u).astype(BF16)
    y = jnp.dot(a, wd_ref[...], preferred_element_type=F32)
    o_ref[...] = x + 0.5 * _rms(y, gpost_ref[...])


def _ffn(x, gpre, gpost, wg, wu, wd):
    m = x.shape[0]
    row = pl.BlockSpec((ROW_TILE, D_MODEL), lambda i: (i, 0))
    return pl.pallas_call(
        _ffn_kernel,
        grid=(m // ROW_TILE,),
        in_specs=[row, _const_spec((1, D_MODEL)), _const_spec((1, D_MODEL)),
                  _const_spec((D_MODEL, D_FF)), _const_spec((D_MODEL, D_FF)), _const_spec((D_FF, D_MODEL))],
        out_specs=row,
        out_shape=jax.ShapeDtypeStruct((m, D_MODEL), F32),
        compiler_params=_cparams("arbitrary"),
        name="ffn",
    )(x, gpre, gpost, wg, wu, wd)


def _inproj_kernel(h_ref, g_ref, w_ref, bf_ref, u_ref, rw_ref, fq_ref, fk_ref, fv_ref, fkb_ref, fvb_ref,
                   lf_ref, mq_ref, mk_ref, mv_ref, mkb_ref, mvb_ref, gate_ref):
    u = _rms(h_ref[...], g_ref[...])
    u_ref[...] = u
    ub = u.astype(BF16)

    def proj(lo, n):
        return jnp.dot(ub, w_ref[:, lo:lo + n], preferred_element_type=F32)

    rw_ref[...] = proj(0, RW_PACK)
    o = RW_PACK
    fq_ref[...] = proj(o, WIDTH)
    fk = proj(o + WIDTH, WIDTH)
    fv = proj(o + 2 * WIDTH, WIDTH)
    fk_ref[...] = fk
    fv_ref[...] = fv
    fkb_ref[...] = fk.astype(BF16)
    fvb_ref[...] = fv.astype(BF16)
    o += 3 * WIDTH
    mq_ref[...] = proj(o, WIDTH)
    mk = proj(o + WIDTH, WIDTH)
    mv = proj(o + 2 * WIDTH, WIDTH)
    mk_ref[...] = mk
    mv_ref[...] = mv
    mkb_ref[...] = mk.astype(BF16)
    mvb_ref[...] = mv.astype(BF16)
    o += 3 * WIDTH
    gate_ref[...] = _sigmoid(proj(o, 3 * D_MODEL))
    o += 3 * D_MODEL
    lf_ref[...] = -_softplus(-(proj(o, LANES) + bf_ref[...]))


def _inproj(h, g, w_pack, bf_pad):
    m = h.shape[0]

    def row(n):
        return pl.BlockSpec((ROW_TILE, n), lambda i: (i, 0))

    widths = [(D_MODEL, F32), (RW_PACK, F32), (WIDTH, F32), (WIDTH, F32), (WIDTH, F32), (WIDTH, BF16), (WIDTH, BF16),
              (LANES, F32), (WIDTH, F32), (WIDTH, F32), (WIDTH, F32), (WIDTH, BF16), (WIDTH, BF16), (3 * D_MODEL, F32)]
    return pl.pallas_call(
        _inproj_kernel,
        grid=(m // ROW_TILE,),
        in_specs=[row(D_MODEL), _const_spec((1, D_MODEL)), _const_spec((D_MODEL, IN_PACK)), _const_spec((1, LANES))],
        out_specs=[row(n) for n, _ in widths],
        out_shape=[jax.ShapeDtypeStruct((m, n), dt) for n, dt in widths],
        compiler_params=_cparams("arbitrary"),
        name="inproj",
    )(h, g, w_pack, bf_pad)


def _mm_kernel(a_ref, b_ref, o_ref):
    o_ref[...] = jnp.dot(a_ref[...].astype(BF16), b_ref[...], preferred_element_type=F32)


def _mm(a, b):
    return pl.pallas_call(
        _mm_kernel,
        out_shape=jax.ShapeDtypeStruct((a.shape[0], b.shape[1]), F32),
        compiler_params=pltpu.CompilerParams(vmem_limit_bytes=VMEM_LIMIT),
        name="shift_proj",
    )(a, b)


def _rwkv_kernel(rw_ref, prev_ref, s0_ref, mu_ref, w0_ref, wup_ref, a0_ref, aup_ref, gup_ref, kk_ref, ka_ref,
                 rk_ref, lnw_ref, lnb_ref, bd_ref, o_ref, sout_ref, last_ref, s_ref, *, t_real):
    c = pl.program_id(1)
    C = RW_CHUNK

    @pl.when(c == 0)
    def _():
        last_ref[...] = prev_ref[...]
        s_ref[...] = s0_ref[...]

    rw = rw_ref[...]
    if t_real < C:
        rw = jnp.concatenate([rw, jnp.zeros((C - t_real, RW_PACK), F32)], axis=0)
    row = lax.broadcasted_iota(jnp.int32, (C, 1), 0)
    rw_prev = jnp.where(row == 0, last_ref[...], pltpu.roll(rw, 1, 0))
    last_ref[...] = rw[C - 1:C, :]
    xm = rw + (rw_prev - rw) * mu_ref[...]
    if t_real < C:
        xm = jnp.where(row < t_real, xm, 0.0)
    r = xm[:, 0:WIDTH]
    k = xm[:, WIDTH:2 * WIDTH]
    v = xm[:, 2 * WIDTH:3 * WIDTH]
    wd = xm[:, 3 * WIDTH:3 * WIDTH + LANES]
    ad = xm[:, 3 * WIDTH + LANES:3 * WIDTH + 2 * LANES]
    gd = xm[:, 3 * WIDTH + 2 * LANES:RW_PACK]
    bd = bd_ref[...]

    z = w0_ref[...] + _dot(jnp.tanh(wd), wup_ref[...])
    lw = -jnp.exp(-_softplus(-z) - 0.5)
    if t_real < C:
        lw = jnp.where(row < t_real, lw, 0.0)
    ag = _sigmoid(a0_ref[...] + _dot(ad, aup_ref[...]))
    g = _dot(_sigmoid(gd), gup_ref[...])
    kk = k * kk_ref[...]
    kk = kk / jnp.maximum(jnp.sqrt(_dot_exact_rhs(kk * kk, bd)), 1e-12)
    kh = k * (1.0 + (ag - 1.0) * ka_ref[...])
    a_ = -kk
    b_ = kk * ag

    ti = lax.broadcasted_iota(jnp.int32, (C, C), 0)
    si = lax.broadcasted_iota(jnp.int32, (C, C), 1)
    incl = si <= ti
    strict = si < ti
    cum = _dot_exact_lhs(incl.astype(BF16), lw)
    tot = cum[C - 1:C, :]
    p_end = jnp.exp(tot - cum)
    p_inv = jnp.exp(-cum)
    at = a_ * jnp.exp(cum - lw)
    rt = r * jnp.exp(cum)
    bt = (b_ * p_inv).astype(BF16)
    kt = (kh * p_inv).astype(BF16)
    bh = b_ * p_end
    khat = kh * p_end
    vb = v.astype(BF16)
    eye = (si == ti).astype(F32)
    lane = lax.broadcasted_iota(jnp.int32, (1, LANES), 1)

    ua_parts, uv_parts, qy_parts, yc_parts = [], [], [], []
    for hp in range(N_HEADS // 2):
        sl = slice(hp * LANES, (hp + 1) * LANES)
        at_p, rt_p, bt_p, kt_p, v_p = at[:, sl], rt[:, sl], bt[:, sl], kt[:, sl], vb[:, sl]
        ua_p = uv_p = qy_p = yc_p = None
        for hh in range(2):
            mh = (lane < HEAD_DIM) if hh == 0 else (lane >= HEAD_DIM)
            at_h = jnp.where(mh, at_p, 0.0).astype(BF16)
            rt_h = jnp.where(mh, rt_p, 0.0).astype(BF16)
            a_ab = jnp.where(strict, _dot_nt(at_h, bt_p), 0.0)
            a_ak = jnp.where(strict, _dot_nt(at_h, kt_p), 0.0)
            a_rb = jnp.where(incl, _dot_nt(rt_h, bt_p), 0.0)
            a_rk = jnp.where(incl, _dot_nt(rt_h, kt_p), 0.0)
            tm = eye + a_ab
            apow = a_ab
            n = 2
            while n < C:
                apow = _dot(apow, apow)
                tm = tm + _dot(tm, apow)
                n *= 2
            ua = _dot(tm, at_p)
            uv = _dot(tm, _dot(a_ak, v_p))
            qy = rt_p + _dot(a_rb, ua)
            yc = _dot(a_rb, uv) + _dot(a_rk, v_p)
            if hh == 0:
                ua_p, uv_p, qy_p, yc_p = ua, uv, qy, yc
            else:
                ua_p, uv_p = jnp.where(mh, ua, ua_p), jnp.where(mh, uv, uv_p)
                qy_p, yc_p = jnp.where(mh, qy, qy_p), jnp.where(mh, yc, yc_p)
        ua_parts.append(ua_p); uv_parts.append(uv_p); qy_parts.append(qy_p); yc_parts.append(yc_p)
    ua_all = jnp.concatenate(ua_parts, axis=1)
    uv_all = jnp.concatenate(uv_parts, axis=1)
    qy_all = jnp.concatenate(qy_parts, axis=1)
    yc_all = jnp.concatenate(yc_parts, axis=1)

    s0 = s_ref[...]
    y = _dot_nt(qy_all, s0) + yc_all
    bdm = bd > 0
    gc = jnp.where(bdm, _dot_tn(ua_all, bh), 0.0)
    hm = jnp.where(bdm, _dot_tn(uv_all, bh) + _dot_tn(v, khat), 0.0)
    s_new = s0 * jnp.exp(tot) + _dot(s0, gc) + hm
    s_ref[...] = s_new
    sout_ref[...] = s_new

    inv_n = 1.0 / HEAD_DIM
    mean = _dot_exact_rhs(y, bd) * inv_n
    yd = y - mean
    var = _dot_exact_rhs(yd * yd, bd) * inv_n
    yn = yd * lax.rsqrt(var + GN_EPS) * lnw_ref[...] + lnb_ref[...]
    bonus = _dot_exact_rhs(r * kh * rk_ref[...], bd) * v
    o = (yn + bonus) * g
    o_ref[...] = o[:t_real] if t_real < C else o


def _rwkv(rw, prev, s0bd, params, bd, *, n_seq, t_seq, row0):
    t_real = min(t_seq, RW_CHUNK)
    n_chunk = max(t_seq // RW_CHUNK, 1)
    blk0 = row0 // t_real

    def vec(n):
        return _const_spec((1, n))

    in_specs = [pl.BlockSpec((t_real, RW_PACK), lambda b, c: (blk0 + b * n_chunk + c, 0)),
                pl.BlockSpec((None, 1, RW_PACK), lambda b, c: (b, 0, 0)),
                pl.BlockSpec((None, WIDTH, WIDTH), lambda b, c: (b, 0, 0)),
                vec(RW_PACK), vec(WIDTH), _const_spec((LANES, WIDTH)), vec(WIDTH), _const_spec((LANES, WIDTH)),
                _const_spec((G_LORA, WIDTH)), vec(WIDTH), vec(WIDTH), vec(WIDTH), vec(WIDTH), vec(WIDTH),
                _const_spec((WIDTH, WIDTH))]
    return pl.pallas_call(
        functools.partial(_rwkv_kernel, t_real=t_real),
        grid=(n_seq, n_chunk),
        in_specs=in_specs,
        out_specs=[pl.BlockSpec((t_real, WIDTH), lambda b, c: (b * n_chunk + c, 0)),
                   pl.BlockSpec((None, WIDTH, WIDTH), lambda b, c: (b, 0, 0))],
        out_shape=[jax.ShapeDtypeStruct((n_seq * t_seq, WIDTH), F32),
                   jax.ShapeDtypeStruct((n_seq, WIDTH, WIDTH), F32)],
        scratch_shapes=[pltpu.VMEM((1, RW_PACK), F32), pltpu.VMEM((WIDTH, WIDTH), F32)],
        compiler_params=_cparams("arbitrary", "arbitrary"),
        name="rwkv",
    )(rw, prev, s0bd, *params, bd)


def _cumsum_kernel(x_ref, tri_ref, blk_ref, o_ref):
    w = _dot_exact_rhs(x_ref[...], tri_ref[...])
    tot = jnp.broadcast_to(w[:, LANES - 1:LANES], w.shape)
    o_ref[...] = w + _dot_exact_lhs(blk_ref[...], tot)


def _cumsum_rows(x, tri, blk):
    return pl.pallas_call(
        _cumsum_kernel,
        out_shape=jax.ShapeDtypeStruct(x.shape, F32),
        compiler_params=pltpu.CompilerParams(vmem_limit_bytes=VMEM_LIMIT),
        name="fox_cumsum",
    )(x, tri, blk)


def _kmean_kernel(k_ref, o_ref):
    n = k_ref.shape[0] // MOBA_BLOCK
    o_ref[...] = jnp.sum(k_ref[...].reshape(n, MOBA_BLOCK, WIDTH), axis=1) * (1.0 / MOBA_BLOCK)


def _kmean(k, n_rows):
    per = 8
    nb = n_rows // MOBA_BLOCK
    return pl.pallas_call(
        _kmean_kernel,
        grid=(nb // per,),
        in_specs=[pl.BlockSpec((per * MOBA_BLOCK, WIDTH), lambda i: (i, 0))],
        out_specs=pl.BlockSpec((per, WIDTH), lambda i: (i, 0)),
        out_shape=jax.ShapeDtypeStruct((nb, WIDTH), F32),
        compiler_params=_cparams("arbitrary"),
        name="moba_kmean",
    )(k)


def _top3(bs, valid, own):
    lane = lax.broadcasted_iota(jnp.int32, bs.shape, 1).astype(F32)
    rem = jnp.where(valid, bs, NEG)
    sel = jnp.zeros(bs.shape, F32)
    for r in range(MOBA_TOPK):
        mx = jnp.max(rem, axis=1, keepdims=True)
        idx = jnp.min(jnp.where(rem == mx, lane, float(bs.shape[1])), axis=1, keepdims=True)
        pick = lane == idx
        sel = jnp.where(pick, jnp.where(own > r, 1.0, 0.0), sel)
        rem = jnp.where(pick, -jnp.inf, rem)
    return sel


def _online(s, v2, m, l, acc):
    m_new = jnp.maximum(m, jnp.max(s, axis=1, keepdims=True))
    alpha = jnp.exp(m - m_new)
    p = jnp.exp(s - m_new)
    l = alpha * l + jnp.sum(p, axis=1, keepdims=True)
    acc = alpha * acc + jnp.dot(p.astype(BF16), v2, preferred_element_type=F32)
    return m_new, l, acc


def _prompt_attn_kernel(*refs, mode):
    if mode == "fox":
        q_ref, k_ref, v_ref, c_ref, ct_ref, o_ref = refs
    else:
        slopes_ref, q_ref, k_ref, v_ref, kmean_ref, o_ref = refs
    T = ATT_TILE
    hp = pl.program_id(0)
    i = pl.program_id(1)
    lane = lax.broadcasted_iota(jnp.int32, (1, LANES), 1)
    lo = lane < HEAD_DIM
    q = q_ref[...]
    qs = q * ATTN_SCALE
    qh = (jnp.where(lo, qs, 0.0).astype(BF16), jnp.where(lo, 0.0, qs).astype(BF16))
    ri = lax.broadcasted_iota(jnp.int32, (T, T), 0)
    ci = lax.broadcasted_iota(jnp.int32, (T, T), 1)
    causal = ci <= ri

    if mode == "fox":
        lane8 = lax.broadcasted_iota(jnp.int32, (1, N_HEADS), 1)
        cblk = c_ref[...]
        cq = [jnp.sum(jnp.where(lane8 == 2 * hp + hh, cblk, 0.0), axis=1, keepdims=True) for hh in range(2)]
    else:
        d0 = (ri - ci).astype(F32)
        slope = [slopes_ref[2 * hp + hh] for hh in range(2)]
        sd0 = [slope[hh] * d0 for hh in range(2)]
        nb = kmean_ref.shape[0]
        kmean = kmean_ref[...]
        blane = lax.broadcasted_iota(jnp.int32, (1, nb), 1)
        sel = []
        for hh in range(2):
            q_h = jnp.where(lo, q, 0.0) if hh == 0 else jnp.where(lo, 0.0, q)
            sel.append(_top3(_dot_nt_hi(q_h, kmean), blane < i, i))

    def scores(j, hh, k2, diagonal):
        s = _dot_nt(qh[hh], k2)
        if mode == "fox":
            ck = ct_ref[pl.ds(2 * hp + hh, 1), pl.ds(pl.multiple_of(j * T, T), T)]
            s = s + cq[hh] - ck
        else:
            base = ((i - j) * T).astype(F32)
            s = (s - slope[hh] * base) - sd0[hh]
            if not diagonal:
                picked = jnp.sum(jnp.where(blane == j, sel[hh], 0.0), axis=1, keepdims=True) > 0.5
                s = jnp.where(picked, s, NEG)
        if diagonal:
            s = jnp.where(causal, s, NEG)
        return s

    def tile(j, carry, diagonal):
        off = pl.multiple_of(j * T, T)
        k2 = k_ref[pl.ds(off, T), :]
        v2 = v_ref[pl.ds(off, T), :]
        out = []
        for hh in range(2):
            m, l, acc = carry[hh]
            out.append(_online(scores(j, hh, k2, diagonal), v2, m, l, acc))
        return tuple(out)

    init = tuple((jnp.full((T, 1), NEG, F32), jnp.zeros((T, 1), F32), jnp.zeros((T, LANES), F32)) for _ in range(2))
    carry = tile(i, init, True)
    carry = lax.fori_loop(0, i, lambda j, cr: tile(j, cr, False), carry)
    (_, l_a, acc_a), (_, l_b, acc_b) = carry
    o_ref[...] = jnp.where(lo, acc_a / l_a, acc_b / l_b)


def _prompt_attn(mode, q, kb, vb, n_rows, *extra):
    nq = n_rows // ATT_TILE
    qspec = pl.BlockSpec((ATT_TILE, LANES), lambda hp, i, *_: (i, hp))
    kvspec = pl.BlockSpec((n_rows, LANES), lambda hp, i, *_: (0, hp))
    if mode == "fox":
        c, ct = extra
        in_specs = [qspec, kvspec, kvspec, pl.BlockSpec((ATT_TILE, N_HEADS), lambda hp, i: (i, 0)),
                    _const_spec((N_HEADS, n_rows))]
        args = (q, kb, vb, c, ct)
        n_prefetch = 0
    else:
        slopes, kmean = extra
        in_specs = [qspec, kvspec, kvspec, pl.BlockSpec((kmean.shape[0], LANES), lambda hp, i, *_: (0, hp))]
        args = (slopes, q, kb, vb, kmean)
        n_prefetch = 1
    return pl.pallas_call(
        functools.partial(_prompt_attn_kernel, mode=mode),
        grid_spec=pltpu.PrefetchScalarGridSpec(
            num_scalar_prefetch=n_prefetch, grid=(N_HEADS // 2, nq), in_specs=in_specs, out_specs=qspec),
        out_shape=jax.ShapeDtypeStruct((n_rows, WIDTH), F32),
        compiler_params=_cparams("arbitrary", "arbitrary"),
        name=mode + "_prompt",
    )(*args)


def _head_rows(x8):
    return jnp.concatenate([jnp.broadcast_to(x8[h:h + 1, :], (8, x8.shape[1])) for h in range(N_HEADS)], axis=0)


def _block_diag_q(q, scale):
    lane = lax.broadcasted_iota(jnp.int32, (1, WIDTH), 1)
    return jnp.concatenate([jnp.where(lane // HEAD_DIM == h, q * scale, 0.0) for h in range(N_HEADS)], axis=0)


def _sample_select_kernel(pt_ref, q_ref, knew_ref, *refs):
    k_refs = refs[:PAGES_PER_STEP]
    sel_ref = refs[PAGES_PER_STEP]
    kmean_ref = refs[PAGES_PER_STEP + 1]
    p = pl.program_id(1)
    per_blk = MOBA_BLOCK // PAGE

    @pl.when(p == 0)
    def _():
        kmean_ref[...] = jnp.zeros(kmean_ref.shape, F32)

    for i in range(PAGES_PER_STEP // per_blk):
        tot = sum(jnp.sum(k_refs[per_blk * i + t][...], axis=0, keepdims=True) for t in range(per_blk))
        kmean_ref[pl.ds(p * (PAGES_PER_STEP // per_blk) + i, 1), :] = tot * (1.0 / MOBA_BLOCK)

    @pl.when(p == pl.num_programs(1) - 1)
    def _():
        n_past = pl.num_programs(1) * (PAGES_PER_STEP // per_blk)
        kmean_ref[pl.ds(n_past, 1), :] = jnp.sum(knew_ref[...], axis=0, keepdims=True) * (1.0 / MOBA_BLOCK)
        bs = _dot_nt_hi(_block_diag_q(q_ref[...], 1.0), kmean_ref[...])
        blane = lax.broadcasted_iota(jnp.int32, (1, LANES), 1)
        sel_ref[...] = _top3(bs, blane < n_past, n_past)


def _page_spec(layer, n_pages, i, reverse):
    def index(b, p, pt):
        pg = p * PAGES_PER_STEP + i
        if reverse:
            pg = n_pages - 1 - pg
        return (layer, pt[b, pg], 0, 0)
    return pl.BlockSpec((None, None, PAGE, WIDTH), index)


def _sample_select(page_table, q, knew, cache_k, layer, row0):
    n_seq, n_pages = page_table.shape
    t = knew.shape[0] // n_seq
    tok = pl.BlockSpec((t, WIDTH), lambda b, p, pt: (row0 // t + b, 0))
    return pl.pallas_call(
        _sample_select_kernel,
        grid_spec=pltpu.PrefetchScalarGridSpec(
            num_scalar_prefetch=1, grid=(n_seq, n_pages // PAGES_PER_STEP),
            in_specs=[tok, pl.BlockSpec((t, WIDTH), lambda b, p, pt: (b, 0))]
            + [_page_spec(layer, n_pages, i, False) for i in range(PAGES_PER_STEP)],
            out_specs=pl.BlockSpec((None, N_HEADS * t, LANES), lambda b, p, pt: (b, 0, 0)),
            scratch_shapes=[pltpu.VMEM((LANES, WIDTH), F32)]),
        out_shape=jax.ShapeDtypeStruct((n_seq, N_HEADS * t, LANES), F32),
        compiler_params=_cparams("arbitrary", "arbitrary"),
        name="moba_sample_select",
    )(page_table, q, knew, *([cache_k] * PAGES_PER_STEP))


def _sample_attn_kernel(*refs, mode, n_pages):
    P = PAGES_PER_STEP
    if mode == "fox":
        pt_ref, q_ref, knew_ref, vnew_ref, lnew_ref = refs[:5]
        rest = refs[5:]
        lf_refs, rest = rest[:P], rest[P:]
    else:
        pt_ref, slopes_ref, q_ref, knew_ref, vnew_ref, sel_ref = refs[:6]
        rest = refs[6:]
    k_refs, v_refs, rest = rest[:P], rest[P:2 * P], rest[2 * P:]
    o_ref, qbd_ref, m_ref, l_ref, acc_ref, carry_ref, cq_ref = rest
    p = pl.program_id(1)
    R = qbd_ref.shape[0]
    t_new = R // N_HEADS
    lane = lax.broadcasted_iota(jnp.int32, (1, LANES), 1)
    trow = lax.broadcasted_iota(jnp.int32, (R, 1), 0) % t_new
    if mode == "moba":
        srow = jnp.concatenate([jnp.full((t_new, 1), slopes_ref[h], F32) for h in range(N_HEADS)], axis=0)

    def update(s, vpage):
        m, l, acc = _online(s, vpage.astype(BF16), m_ref[...], l_ref[...], acc_ref[...])
        m_ref[...] = m
        l_ref[...] = l
        acc_ref[...] = acc

    @pl.when(p == 0)
    def _():
        qbd_ref[...] = _block_diag_q(q_ref[...], ATTN_SCALE).astype(BF16)
        m_ref[...] = jnp.full(m_ref.shape, NEG, F32)
        l_ref[...] = jnp.zeros(l_ref.shape, F32)
        acc_ref[...] = jnp.zeros(acc_ref.shape, F32)
        pad = jnp.zeros((PAGE - t_new, WIDTH), F32)
        s = _dot_nt(qbd_ref[...], jnp.concatenate([knew_ref[...], pad], axis=0))
        if mode == "fox":
            x = lnew_ref[...]
            d = 1
            while d < t_new:
                x = x + jnp.where(lane >= d, pltpu.roll(x, d, 1), 0.0)
                d *= 2
            crep = _head_rows(x)
            cq = jnp.sum(jnp.where(lane == trow, crep, 0.0), axis=1, keepdims=True)
            cq_ref[...] = cq
            carry_ref[...] = jnp.zeros(carry_ref.shape, F32)
            s = s + cq - crep
        else:
            s = s - srow * (trow - lane).astype(F32)
        update(jnp.where(lane <= trow, s, NEG), jnp.concatenate([vnew_ref[...], pad], axis=0))

    for i in range(P):
        pg = n_pages - 1 - (p * P + i)
        s = _dot_nt(qbd_ref[...], k_refs[i][...])
        if mode == "fox":
            lf = lf_refs[i][...]
            x = lf
            d = 1
            while d < PAGE:
                x = x + jnp.where(lane < PAGE - d, pltpu.roll(x, PAGE - d, 1), 0.0)
                d *= 2
            carry = carry_ref[...]
            bias = (x - lf) + carry
            carry_ref[...] = carry + x[:, 0:1]
            s = s + _head_rows(bias) + cq_ref[...]
        else:
            dist = (n_pages * PAGE + trow) - (pg * PAGE + lane)
            s = s - srow * dist.astype(F32)
            picked = jnp.sum(jnp.where(lane == pg // (MOBA_BLOCK // PAGE), sel_ref[...], 0.0), axis=1, keepdims=True) > 0.5
            s = jnp.where(picked, s, NEG)
        update(s, v_refs[i][...])

    @pl.when(p == pl.num_programs(1) - 1)
    def _():
        res = acc_ref[...] / l_ref[...]
        wl = lax.broadcasted_iota(jnp.int32, (1, WIDTH), 1)
        out = jnp.zeros((t_new, WIDTH), F32)
        for h in range(N_HEADS):
            out = jnp.where(wl // HEAD_DIM == h, res[h * t_new:(h + 1) * t_new, :], out)
        o_ref[...] = out


def _sample_attn(mode, page_table, q, knew, vnew, cache_k, cache_v, layer, row0, *extra):
    n_seq, n_pages = page_table.shape
    t = knew.shape[0] // n_seq
    R = N_HEADS * t
    P = PAGES_PER_STEP
    nsp = 1 if mode == "fox" else 2
    qtok = pl.BlockSpec((t, WIDTH), lambda b, p, *_: (row0 // t + b, 0))
    tok = pl.BlockSpec((t, WIDTH), lambda b, p, *_: (b, 0))
    pages = [_page_spec(layer, n_pages, i, True) for i in range(P)]
    if nsp == 2:
        pages = [pl.BlockSpec(s.block_shape, (lambda f: lambda b, p, pt, sl: f(b, p, pt))(s.index_map)) for s in pages]
    if mode == "fox":
        lnew_t, lf_t = extra
        def lf_spec(i):
            return pl.BlockSpec((None, N_HEADS, PAGE), lambda b, p, pt: (pt[b, n_pages - 1 - (p * P + i)], 0, 0))
        in_specs = ([qtok, tok, tok, pl.BlockSpec((None, N_HEADS, LANES), lambda b, p, pt: (b, 0, 0))]
                    + [lf_spec(i) for i in range(P)] + pages + pages)
        args = (page_table, q, knew, vnew, lnew_t, *([lf_t] * P), *([cache_k] * P), *([cache_v] * P))
    else:
        slopes, sel = extra
        in_specs = [qtok, tok, tok, pl.BlockSpec((None, R, LANES), lambda b, p, pt, sl: (b, 0, 0))] + pages + pages
        args = (page_table, slopes, q, knew, vnew, sel, *([cache_k] * P), *([cache_v] * P))
    return pl.pallas_call(
        functools.partial(_sample_attn_kernel, mode=mode, n_pages=n_pages),
        grid_spec=pltpu.PrefetchScalarGridSpec(
            num_scalar_prefetch=nsp, grid=(n_seq, n_pages // P), in_specs=in_specs, out_specs=tok,
            scratch_shapes=[pltpu.VMEM((R, WIDTH), BF16), pltpu.VMEM((R, 1), F32), pltpu.VMEM((R, 1), F32),
                            pltpu.VMEM((R, WIDTH), F32), pltpu.VMEM((N_HEADS, 1), F32), pltpu.VMEM((R, 1), F32)]),
        out_shape=jax.ShapeDtypeStruct((n_seq * t, WIDTH), F32),
        compiler_params=_cparams("arbitrary", "arbitrary"),
        name=mode + "_sample",
    )(*args)


def _merge_kernel(h_ref, gate_ref, orw_ref, ofox_ref, omoba_ref, wr_ref, wf_ref, wm_ref, wo_ref, g_ref, o_ref):
    m = (gate_ref[:, 0:D_MODEL] * _dot(orw_ref[...], wr_ref[...])
         + gate_ref[:, D_MODEL:2 * D_MODEL] * _dot(ofox_ref[...], wf_ref[...])
         + gate_ref[:, 2 * D_MODEL:3 * D_MODEL] * _dot(omoba_ref[...], wm_ref[...]))
    o_ref[...] = h_ref[...] + _rms(_dot(m, wo_ref[...]), g_ref[...])


def _merge(h, gates, o_rw, o_fox, o_moba, wr, wf, wm, wo, g):
    m = h.shape[0]

    def row(n):
        return pl.BlockSpec((ROW_TILE, n), lambda i: (i, 0))

    wspec = _const_spec((WIDTH, D_MODEL))
    return pl.pallas_call(
        _merge_kernel,
        grid=(m // ROW_TILE,),
        in_specs=[row(D_MODEL), row(3 * D_MODEL), row(WIDTH), row(WIDTH), row(WIDTH), wspec, wspec, wspec,
                  _const_spec((D_MODEL, D_MODEL)), _const_spec((1, D_MODEL))],
        out_specs=row(D_MODEL),
        out_shape=jax.ShapeDtypeStruct((m, D_MODEL), F32),
        compiler_params=_cparams("arbitrary"),
        name="merge",
    )(h, gates, o_rw, o_fox, o_moba, wr, wf, wm, wo, g)


def _pack_w_in(w_in, b_forget):
    o = 0
    def take(n):
        nonlocal o
        s = w_in[:, o:o + n]
        o += n
        return s
    rkv = take(3 * WIDTH)
    wd, ad, gd = take(W_LORA), take(A_LORA), take(G_LORA)
    fox = take(3 * WIDTH)
    fl = take(N_HEADS)
    moba = take(3 * WIDTH)
    gates = take(3 * D_MODEL)
    z = lambda n: jnp.zeros((D_MODEL, n), w_in.dtype)
    packed = jnp.concatenate([rkv, wd, z(LANES - W_LORA), ad, z(LANES - A_LORA), gd, fox, moba, gates,
                              fl, z(LANES - N_HEADS)], axis=1)
    bf = jnp.concatenate([b_forget, jnp.zeros((LANES - N_HEADS,), F32)])[None, :]
    return packed.astype(BF16), bf


def _pad_rows(w, n):
    return jnp.concatenate([w, jnp.zeros((n - w.shape[0], w.shape[1]), w.dtype)], axis=0)


def _pack_mu(mu):
    z = jnp.zeros((LANES - W_LORA,), F32)
    return jnp.concatenate([mu[:3 * WIDTH], mu[3 * WIDTH:3 * WIDTH + W_LORA], z,
                            mu[3 * WIDTH + W_LORA:3 * WIDTH + W_LORA + A_LORA], z,
                            mu[3 * WIDTH + W_LORA + A_LORA:]])[None, :]


def _block_diag_state(s):
    b = s.shape[0]
    eye = jnp.eye(N_HEADS, dtype=s.dtype)
    return jnp.einsum('bhij,hg->bhigj', s, eye).reshape(b, WIDTH, WIDTH)


def _unblock_state(s):
    b = s.shape[0]
    s5 = s.reshape(b, N_HEADS, HEAD_DIM, N_HEADS, HEAD_DIM)
    return jnp.stack([s5[:, h, :, h, :] for h in range(N_HEADS)], axis=1)


def kernel(x_prompt, x_sample, cache_fox_k, cache_fox_v, cache_fox_logf, cache_moba_k, cache_moba_v, state_rwkv_wkv, state_rwkv_shift, page_table, norms, w_in, b_forget, rwkv_mu, rwkv_w0, rwkv_w_up, rwkv_a0, rwkv_a_up, rwkv_g_up, rwkv_k_k, rwkv_k_a, rwkv_r_k, rwkv_ln_w, rwkv_ln_b, w_proj_rwkv, w_proj_fox, w_proj_moba, w_out, ffn_gate, ffn_up, ffn_down):
    depth = norms.shape[0]
    n_p = x_prompt.shape[0] * x_prompt.shape[1]
    n_seq, t_s = x_sample.shape[0], x_sample.shape[1]
    n_s = n_seq * t_s
    n_pool = cache_fox_k.shape[1]
    assert x_prompt.shape[0] == 1 and n_p % ROW_TILE == 0 and n_s % ROW_TILE == 0 and n_p % (8 * MOBA_BLOCK) == 0

    x = jnp.concatenate([x_prompt.reshape(n_p, D_MODEL), x_sample.reshape(n_s, D_MODEL)], axis=0)
    slopes = 2.0 ** (-8.0 * jnp.arange(1, N_HEADS + 1, dtype=F32) / N_HEADS)
    hid = jnp.arange(WIDTH) // HEAD_DIM
    bd = (hid[:, None] == hid[None, :]).astype(BF16)
    ar = jnp.arange(LANES)
    tri = (ar[:, None] <= ar[None, :]).astype(BF16)
    n_rows_c = N_HEADS * n_p // LANES
    rr = jnp.arange(n_rows_c)
    blk = ((rr[:, None] // (n_p // LANES) == rr[None, :] // (n_p // LANES)) & (rr[None, :] < rr[:, None])).astype(BF16)
    cfk = cache_fox_k.reshape(depth, n_pool, PAGE, WIDTH)
    cfv = cache_fox_v.reshape(depth, n_pool, PAGE, WIDTH)
    cmk = cache_moba_k.reshape(depth, n_pool, PAGE, WIDTH)
    cmv = cache_moba_v.reshape(depth, n_pool, PAGE, WIDTH)
    vec = lambda a: a.reshape(1, -1)

    outs_p, outs_s = [], []
    for l in range(depth):
        bfc = lambda a: a.astype(BF16)
        h = _ffn(x, vec(norms[l, 0]), vec(norms[l, 1]), bfc(ffn_gate[l, 0]), bfc(ffn_up[l, 0]), bfc(ffn_down[l, 0]))
        w_pack, bf_pad = _pack_w_in(w_in[l], b_forget[l])
        (u, rw, fq, fk, fv, fkb, fvb, logf, mq, mk, mv, mkb, mvb, gates) = _inproj(h, vec(norms[l, 2]), w_pack, bf_pad)

        prev_u = jnp.concatenate([jnp.zeros((8, D_MODEL), F32), state_rwkv_shift[l]], axis=0)
        prev_rw = _mm(prev_u, w_pack[:, :RW_PACK])
        rparams = (_pack_mu(rwkv_mu[l]), vec(rwkv_w0[l]), bfc(_pad_rows(rwkv_w_up[l], LANES)), vec(rwkv_a0[l]),
                   bfc(_pad_rows(rwkv_a_up[l], LANES)), bfc(rwkv_g_up[l]), vec(rwkv_k_k[l]), vec(rwkv_k_a[l]),
                   vec(rwkv_r_k[l]), vec(rwkv_ln_w[l]), vec(rwkv_ln_b[l]))
        o_rw_p, s_p = _rwkv(rw, prev_rw[0:1][None], jnp.zeros((1, WIDTH, WIDTH), F32), rparams, bd,
                            n_seq=1, t_seq=n_p, row0=0)
        o_rw_s, s_s = _rwkv(rw, prev_rw[8:8 + n_seq][:, None, :], _block_diag_state(state_rwkv_wkv[l]), rparams, bd,
                            n_seq=n_seq, t_seq=t_s, row0=n_p)

        lf_t = logf[:n_p, :N_HEADS].T
        ct = _cumsum_rows(lf_t.reshape(n_rows_c, LANES), tri, blk).reshape(N_HEADS, n_p)
        o_fox_p = _prompt_attn("fox", fq, fkb, fvb, n_p, ct.T, ct)
        lnew_t = jnp.swapaxes(logf[n_p:, :N_HEADS].reshape(n_seq, t_s, N_HEADS), 1, 2)
        lnew_t = jnp.concatenate([lnew_t, jnp.zeros((n_seq, N_HEADS, LANES - t_s), F32)], axis=2)
        lf_pool_t = jnp.swapaxes(cache_fox_logf[l].astype(F32), 1, 2)
        o_fox_s = _sample_attn("fox", page_table, fq, fk[n_p:], fv[n_p:], cfk, cfv, l, n_p, lnew_t, lf_pool_t)

        kmean = _kmean(mk, n_p)
        o_moba_p = _prompt_attn("moba", mq, mkb, mvb, n_p, slopes, kmean)
        sel = _sample_select(page_table, mq, mk[n_p:], cmk, l, n_p)
        o_moba_s = _sample_attn("moba", page_table, mq, mk[n_p:], mv[n_p:], cmk, cmv, l, n_p, slopes, sel)

        cat = lambda a, b: jnp.concatenate([a, b], axis=0)
        h = _merge(h, gates, cat(o_rw_p, o_rw_s), cat(o_fox_p, o_fox_s), cat(o_moba_p, o_moba_s),
                   bfc(w_proj_rwkv[l]), bfc(w_proj_fox[l]), bfc(w_proj_moba[l]), bfc(w_out[l]), vec(norms[l, 3]))
        x = _ffn(h, vec(norms[l, 4]), vec(norms[l, 5]), bfc(ffn_gate[l, 1]), bfc(ffn_up[l, 1]), bfc(ffn_down[l, 1]))

        hd = lambda a, b, t: a.reshape(b, t, N_HEADS, HEAD_DIM)
        outs_p.append((hd(fk[:n_p], 1, n_p), hd(fv[:n_p], 1, n_p), logf[:n_p, :N_HEADS].reshape(1, n_p, N_HEADS),
                       hd(mk[:n_p], 1, n_p), hd(mv[:n_p], 1, n_p), _unblock_state(s_p), u[n_p - 1:n_p]))
        outs_s.append((hd(fk[n_p:], n_seq, t_s), hd(fv[n_p:], n_seq, t_s),
                       logf[n_p:, :N_HEADS].reshape(n_seq, t_s, N_HEADS), hd(mk[n_p:], n_seq, t_s),
                       hd(mv[n_p:], n_seq, t_s), _unblock_state(s_s),
                       u[n_p:].reshape(n_seq, t_s, D_MODEL)[:, -1]))

    st = lambda rows, i: jnp.stack([r[i] for r in rows])
    return (x[:n_p].reshape(x_prompt.shape), x[n_p:].reshape(x_sample.shape),
            *[st(outs_p, i) for i in range(7)], *[st(outs_s, i) for i in range(7)])
```

```python
import functools

import jax
import jax.numpy as jnp
from jax import lax
from jax.experimental import pallas as pl
from jax.experimental.pallas import tpu as pltpu

F32 = jnp.float32
BF16 = jnp.bfloat16

D_MODEL = 1024
HEAD_DIM = 64
N_HEADS = 8
WIDTH = N_HEADS * HEAD_DIM
W_LORA, A_LORA, G_LORA = 64, 64, 128
D_FF = 2816
PAGE = 128
MOBA_BLOCK = 256
MOBA_TOPK = 3
RMS_EPS = 1e-6
GN_EPS = 64e-5
ATTN_SCALE = HEAD_DIM ** -0.5
NEG = -1e30
LANES = 128
RW_PACK = 3 * WIDTH + 3 * LANES
IN_PACK = RW_PACK + 6 * WIDTH + 3 * D_MODEL + LANES
ROW_TILE = 256
RW_CHUNK = 128
ATT_TILE = 256
WIDE = 4
PAGES_PER_STEP = 16
LOG2E = 1.4426950408889634
VMEM_LIMIT = 56 * 1024 * 1024


def _cparams(*sem):
    return pltpu.CompilerParams(dimension_semantics=sem, vmem_limit_bytes=VMEM_LIMIT)


def _dot(a, b):
    return jnp.dot(a.astype(BF16), b.astype(BF16), preferred_element_type=F32)


def _dot_nt(a, b):
    return lax.dot_general(a.astype(BF16), b.astype(BF16), (((1,), (1,)), ((), ())), preferred_element_type=F32)


def _dot_tn(a, b):
    return lax.dot_general(a.astype(BF16), b.astype(BF16), (((0,), (0,)), ((), ())), preferred_element_type=F32)


def _split3(a):
    a1 = a.astype(BF16)
    r1 = a - a1.astype(F32)
    a2 = r1.astype(BF16)
    a3 = (r1 - a2.astype(F32)).astype(BF16)
    return a1, a2, a3


def _dot_exact_rhs(a, b01):
    a1, a2, a3 = _split3(a)
    return _dot(a1, b01) + _dot(a2, b01) + _dot(a3, b01)


def _dot_exact_lhs(a01, b):
    b1, b2, b3 = _split3(b)
    return _dot(a01, b1) + _dot(a01, b2) + _dot(a01, b3)


def _dot_nt_hi(a, b):
    a1, a2, a3 = _split3(a)
    b1, b2, b3 = _split3(b)
    return (_dot_nt(a1, b1) + _dot_nt(a1, b2) + _dot_nt(a2, b1)
            + _dot_nt(a1, b3) + _dot_nt(a2, b2) + _dot_nt(a3, b1))


def _dot_hi(a, b):
    a1, a2, a3 = _split3(a)
    b1, b2, b3 = _split3(b)
    return _dot(a1, b1) + _dot(a1, b2) + _dot(a2, b1) + _dot(a1, b3) + _dot(a2, b2) + _dot(a3, b1)


def _rms(x, g):
    return x * lax.rsqrt(jnp.mean(x * x, axis=-1, keepdims=True) + RMS_EPS) * g


def _sigmoid(x):
    return 1.0 / (1.0 + jnp.exp(-x))


def _softplus(x):
    return jnp.maximum(x, 0.0) + jnp.log1p(jnp.exp(-jnp.abs(x)))


def _const_spec(shape):
    nd = len(shape)
    return pl.BlockSpec(shape, lambda *_: (0,) * nd, pipeline_mode=pl.Buffered(1))


def _ffn_kernel(x_ref, gpre_ref, gpost_ref, wg_ref, wu_ref, wd_ref, o_ref):
    x = x_ref[...]
    xn = _rms(x, gpre_ref[...]).astype(BF16)
    g = jnp.dot(xn, wg_ref[...], preferred_element_type=F32)
    u = jnp.dot(xn, wu_ref[...], preferred_element_type=F32)
    a = (g * _sigmoid(g) * u).astype(BF16)
    y = jnp.dot(a, wd_ref[...], preferred_element_type=F32)
    o_ref[...] = x + 0.5 * _rms(y, gpost_ref[...])


def _ffn(x, gpre, gpost, wg, wu, wd):
    m = x.shape[0]
    row = pl.BlockSpec((ROW_TILE, D_MODEL), lambda i: (i, 0))
    return pl.pallas_call(
        _ffn_kernel,
        grid=(m // ROW_TILE,),
        in_specs=[row, _const_spec((1, D_MODEL)), _const_spec((1, D_MODEL)),
                  _const_spec((D_MODEL, D_FF)), _const_spec((D_MODEL, D_FF)), _const_spec((D_FF, D_MODEL))],
        out_specs=row,
        out_shape=jax.ShapeDtypeStruct((m, D_MODEL), F32),
        compiler_params=_cparams("arbitrary"),
        name="ffn",
    )(x, gpre, gpost, wg, wu, wd)


def _inproj_kernel(h_ref, g_ref, w_ref, bf_ref, u_ref, rw_ref, fq_ref, fk_ref, fv_ref, fkb_ref, fvb_ref,
                   lf_ref, mq_ref, mk_ref, mv_ref, mkb_ref, mvb_ref, gate_ref):
    u = _rms(h_ref[...], g_ref[...])
    u_ref[...] = u
    ub = u.astype(BF16)

    def proj(lo, n):
        return jnp.dot(ub, w_ref[:, lo:lo + n], preferred_element_type=F32)

    rw_ref[...] = proj(0, RW_PACK)
    o = RW_PACK
    fq_ref[...] = proj(o, WIDTH)
    fk = proj(o + WIDTH, WIDTH)
    fv = proj(o + 2 * WIDTH, WIDTH)
    fk_ref[...] = fk
    fv_ref[...] = fv
    fkb_ref[...] = fk.astype(BF16)
    fvb_ref[...] = fv.astype(BF16)
    o += 3 * WIDTH
    mq_ref[...] = proj(o, WIDTH)
    mk = proj(o + WIDTH, WIDTH)
    mv = proj(o + 2 * WIDTH, WIDTH)
    mk_ref[...] = mk
    mv_ref[...] = mv
    mkb_ref[...] = mk.astype(BF16)
    mvb_ref[...] = mv.astype(BF16)
    o += 3 * WIDTH
    gate_ref[...] = _sigmoid(proj(o, 3 * D_MODEL))
    o += 3 * D_MODEL
    lf_ref[...] = -_softplus(-(proj(o, LANES) + bf_ref[...]))


def _inproj(h, g, w_pack, bf_pad):
    m = h.shape[0]

    def row(n):
        return pl.BlockSpec((ROW_TILE, n), lambda i: (i, 0))

    widths = [(D_MODEL, F32), (RW_PACK, F32), (WIDTH, F32), (WIDTH, F32), (WIDTH, F32), (WIDTH, BF16), (WIDTH, BF16),
              (LANES, F32), (WIDTH, F32), (WIDTH, F32), (WIDTH, F32), (WIDTH, BF16), (WIDTH, BF16), (3 * D_MODEL, F32)]
    return pl.pallas_call(
        _inproj_kernel,
        grid=(m // ROW_TILE,),
        in_specs=[row(D_MODEL), _const_spec((1, D_MODEL)), _const_spec((D_MODEL, IN_PACK)), _const_spec((1, LANES))],
        out_specs=[row(n) for n, _ in widths],
        out_shape=[jax.ShapeDtypeStruct((m, n), dt) for n, dt in widths],
        compiler_params=_cparams("arbitrary"),
        name="inproj",
    )(h, g, w_pack, bf_pad)


def _mm_kernel(a_ref, b_ref, o_ref):
    o_ref[...] = jnp.dot(a_ref[...].astype(BF16), b_ref[...], preferred_element_type=F32)


def _mm(a, b):
    return pl.pallas_call(
        _mm_kernel,
        out_shape=jax.ShapeDtypeStruct((a.shape[0], b.shape[1]), F32),
        compiler_params=pltpu.CompilerParams(vmem_limit_bytes=VMEM_LIMIT),
        name="shift_proj",
    )(a, b)


def _rwkv_kernel(rw_ref, prev_ref, s0_ref, mu_ref, w0_ref, wup_ref, a0_ref, aup_ref, gup_ref, kk_ref, ka_ref,
                 rk_ref, lnw_ref, lnb_ref, bd_ref, o_ref, sout_ref, last_ref, s_ref, *, t_real):
    c = pl.program_id(1)
    C = RW_CHUNK

    @pl.when(c == 0)
    def _():
        last_ref[...] = prev_ref[...]
        s_ref[...] = s0_ref[...]

    rw = rw_ref[...]
    if t_real < C:
        rw = jnp.concatenate([rw, jnp.zeros((C - t_real, RW_PACK), F32)], axis=0)
    row = lax.broadcasted_iota(jnp.int32, (C, 1), 0)
    rw_prev = jnp.where(row == 0, last_ref[...], pltpu.roll(rw, 1, 0))
    last_ref[...] = rw[C - 1:C, :]
    xm = rw + (rw_prev - rw) * mu_ref[...]
    if t_real < C:
        xm = jnp.where(row < t_real, xm, 0.0)
    r = xm[:, 0:WIDTH]
    k = xm[:, WIDTH:2 * WIDTH]
    v = xm[:, 2 * WIDTH:3 * WIDTH]
    wd = xm[:, 3 * WIDTH:3 * WIDTH + LANES]
    ad = xm[:, 3 * WIDTH + LANES:3 * WIDTH + 2 * LANES]
    gd = xm[:, 3 * WIDTH + 2 * LANES:RW_PACK]
    bd = bd_ref[...]

    z = w0_ref[...] + _dot(jnp.tanh(wd), wup_ref[...])
    lw = -jnp.exp(-_softplus(-z) - 0.5)
    if t_real < C:
        lw = jnp.where(row < t_real, lw, 0.0)
    ag = _sigmoid(a0_ref[...] + _dot(ad, aup_ref[...]))
    g = _dot(_sigmoid(gd), gup_ref[...])
    kk = k * kk_ref[...]
    kk = kk / jnp.maximum(jnp.sqrt(_dot_exact_rhs(kk * kk, bd)), 1e-12)
    kh = k * (1.0 + (ag - 1.0) * ka_ref[...])
    a_ = -kk
    b_ = kk * ag

    ti = lax.broadcasted_iota(jnp.int32, (C, C), 0)
    si = lax.broadcasted_iota(jnp.int32, (C, C), 1)
    incl = si <= ti
    strict = si < ti
    cum = _dot_exact_lhs(incl.astype(BF16), lw)
    tot = cum[C - 1:C, :]
    p_end = jnp.exp(tot - cum)
    p_inv = jnp.exp(-cum)
    at = a_ * jnp.exp(cum - lw)
    rt = r * jnp.exp(cum)
    bt = (b_ * p_inv).astype(BF16)
    kt = (kh * p_inv).astype(BF16)
    bh = b_ * p_end
    khat = kh * p_end
    vb = v.astype(BF16)
    eye = (si == ti).astype(F32)
    lane = lax.broadcasted_iota(jnp.int32, (1, LANES), 1)

    heads = range(N_HEADS)
    pair = lambda x, h: x[:, (h // 2) * LANES:(h // 2 + 1) * LANES]
    mh = [(lane < HEAD_DIM) if h % 2 == 0 else (lane >= HEAD_DIM) for h in heads]
    at_h = [jnp.where(mh[h], pair(at, h), 0.0).astype(BF16) for h in heads]
    rt_h = [jnp.where(mh[h], pair(rt, h), 0.0).astype(BF16) for h in heads]
    a_ab = [jnp.where(strict, _dot_nt(at_h[h], pair(bt, h)), 0.0) for h in heads]
    a_ak = [jnp.where(strict, _dot_nt(at_h[h], pair(kt, h)), 0.0) for h in heads]
    a_rb = [jnp.where(incl, _dot_nt(rt_h[h], pair(bt, h)), 0.0) for h in heads]
    a_rk = [jnp.where(incl, _dot_nt(rt_h[h], pair(kt, h)), 0.0) for h in heads]
    tm = [eye + a_ab[h] for h in heads]
    apow = a_ab
    n = 2
    while n < C:
        apow = [_dot(apow[h], apow[h]) for h in heads]
        tm = [tm[h] + _dot(tm[h], apow[h]) for h in heads]
        n *= 2
    ua = [_dot(tm[h], pair(at, h)) for h in heads]
    akv = [_dot(a_ak[h], pair(vb, h)) for h in heads]
    uv = [_dot(tm[h], akv[h]) for h in heads]
    qy = [pair(rt, h) + _dot(a_rb[h], ua[h]) for h in heads]
    yc = [_dot(a_rb[h], uv[h]) + _dot(a_rk[h], pair(vb, h)) for h in heads]
    merge = lambda xs: jnp.concatenate([jnp.where(mh[0], xs[h], xs[h + 1]) for h in range(0, N_HEADS, 2)], axis=1)
    ua_all, uv_all, qy_all, yc_all = merge(ua), merge(uv), merge(qy), merge(yc)

    s0 = s_ref[...]
    y = _dot_nt(qy_all, s0) + yc_all
    bdm = bd > 0
    gc = jnp.where(bdm, _dot_tn(ua_all, bh), 0.0)
    hm = jnp.where(bdm, _dot_tn(uv_all, bh) + _dot_tn(v, khat), 0.0)
    s_new = s0 * jnp.exp(tot) + _dot(s0, gc) + hm
    s_ref[...] = s_new
    sout_ref[...] = s_new

    inv_n = 1.0 / HEAD_DIM
    mean = _dot_exact_rhs(y, bd) * inv_n
    yd = y - mean
    var = _dot_exact_rhs(yd * yd, bd) * inv_n
    yn = yd * lax.rsqrt(var + GN_EPS) * lnw_ref[...] + lnb_ref[...]
    bonus = _dot_exact_rhs(r * kh * rk_ref[...], bd) * v
    o = (yn + bonus) * g
    o_ref[...] = o[:t_real] if t_real < C else o


def _rwkv(rw, prev, s0bd, params, bd, *, n_seq, t_seq, row0):
    t_real = min(t_seq, RW_CHUNK)
    n_chunk = max(t_seq // RW_CHUNK, 1)
    blk0 = row0 // t_real

    def vec(n):
        return _const_spec((1, n))

    in_specs = [pl.BlockSpec((t_real, RW_PACK), lambda b, c: (blk0 + b * n_chunk + c, 0)),
                pl.BlockSpec((None, 1, RW_PACK), lambda b, c: (b, 0, 0)),
                pl.BlockSpec((None, WIDTH, WIDTH), lambda b, c: (b, 0, 0)),
                vec(RW_PACK), vec(WIDTH), _const_spec((LANES, WIDTH)), vec(WIDTH), _const_spec((LANES, WIDTH)),
                _const_spec((G_LORA, WIDTH)), vec(WIDTH), vec(WIDTH), vec(WIDTH), vec(WIDTH), vec(WIDTH),
                _const_spec((WIDTH, WIDTH))]
    return pl.pallas_call(
        functools.partial(_rwkv_kernel, t_real=t_real),
        grid=(n_seq, n_chunk),
        in_specs=in_specs,
        out_specs=[pl.BlockSpec((t_real, WIDTH), lambda b, c: (b * n_chunk + c, 0)),
                   pl.BlockSpec((None, WIDTH, WIDTH), lambda b, c: (b, 0, 0))],
        out_shape=[jax.ShapeDtypeStruct((n_seq * t_seq, WIDTH), F32),
                   jax.ShapeDtypeStruct((n_seq, WIDTH, WIDTH), F32)],
        scratch_shapes=[pltpu.VMEM((1, RW_PACK), F32), pltpu.VMEM((WIDTH, WIDTH), F32)],
        compiler_params=_cparams("arbitrary", "arbitrary"),
        name="rwkv",
    )(rw, prev, s0bd, *params, bd)


def _cumsum_kernel(x_ref, tri_ref, blk_ref, o_ref):
    w = _dot_exact_rhs(x_ref[...], tri_ref[...])
    tot = jnp.broadcast_to(w[:, LANES - 1:LANES], w.shape)
    o_ref[...] = (w + _dot_exact_lhs(blk_ref[...], tot)) * LOG2E


def _cumsum_rows(x, tri, blk):
    return pl.pallas_call(
        _cumsum_kernel,
        out_shape=jax.ShapeDtypeStruct(x.shape, F32),
        compiler_params=pltpu.CompilerParams(vmem_limit_bytes=VMEM_LIMIT),
        name="fox_cumsum",
    )(x, tri, blk)


def _kmean_kernel(k_ref, o_ref):
    n = k_ref.shape[0] // MOBA_BLOCK
    o_ref[...] = jnp.sum(k_ref[...].reshape(n, MOBA_BLOCK, WIDTH), axis=1) * (1.0 / MOBA_BLOCK)


def _kmean(k, n_rows):
    per = 8
    nb = n_rows // MOBA_BLOCK
    return pl.pallas_call(
        _kmean_kernel,
        grid=(nb // per,),
        in_specs=[pl.BlockSpec((per * MOBA_BLOCK, WIDTH), lambda i: (i, 0))],
        out_specs=pl.BlockSpec((per, WIDTH), lambda i: (i, 0)),
        out_shape=jax.ShapeDtypeStruct((nb, WIDTH), F32),
        compiler_params=_cparams("arbitrary"),
        name="moba_kmean",
    )(k)


def _top3(bs, valid, own):
    lane = lax.broadcasted_iota(jnp.int32, bs.shape, 1).astype(F32)
    rem = jnp.where(valid, bs, NEG)
    sel = jnp.zeros(bs.shape, F32)
    for r in range(MOBA_TOPK):
        mx = jnp.max(rem, axis=1, keepdims=True)
        idx = jnp.min(jnp.where(rem == mx, lane, float(bs.shape[1])), axis=1, keepdims=True)
        pick = lane == idx
        sel = jnp.where(pick, jnp.where(own > r, 1.0, 0.0), sel)
        rem = jnp.where(pick, -jnp.inf, rem)
    return sel


def _prompt_attn_kernel(*refs, mode):
    if mode == "fox":
        q_ref, k_ref, v_ref, c_ref, ct_ref, o_ref = refs
    else:
        slopes_ref, q_ref, k_ref, v_ref, kmean_ref, o_ref = refs
    T = ATT_TILE
    hp = pl.program_id(0)
    i = pl.program_id(1)
    lane = lax.broadcasted_iota(jnp.int32, (1, LANES), 1)
    lo = lane < HEAD_DIM
    q = q_ref[...]
    qs = q * (ATTN_SCALE * LOG2E)
    qh = (jnp.where(lo, qs, 0.0).astype(BF16), jnp.where(lo, 0.0, qs).astype(BF16))
    ri = lax.broadcasted_iota(jnp.int32, (T, T), 0)
    ci = lax.broadcasted_iota(jnp.int32, (T, T), 1)
    causal = ci <= ri

    if mode == "fox":
        lane8 = lax.broadcasted_iota(jnp.int32, (1, N_HEADS), 1)
        cblk = c_ref[...]
        cq = [jnp.sum(jnp.where(lane8 == 2 * hp + hh, cblk, 0.0), axis=1, keepdims=True) for hh in range(2)]
    else:
        d0 = (ri - ci).astype(F32)
        slope = [slopes_ref[2 * hp + hh] * LOG2E for hh in range(2)]
        sd0 = [slope[hh] * d0 for hh in range(2)]
        nb = kmean_ref.shape[0]
        kmean = kmean_ref[...]
        blane = lax.broadcasted_iota(jnp.int32, (1, nb), 1)
        sel = []
        for hh in range(2):
            q_h = jnp.where(lo, q, 0.0) if hh == 0 else jnp.where(lo, 0.0, q)
            sel.append(_top3(_dot_nt_hi(q_h, kmean), blane < i, i))

    def chunk(j0, nblk, carry, diagonal):
        off = pl.multiple_of(j0 * T, T)
        k2 = k_ref[pl.ds(off, nblk * T), :]
        v2 = v_ref[pl.ds(off, nblk * T), :]
        two = range(2)
        m = [carry[hh][0] for hh in two]
        s = [_dot_nt(qh[hh], k2) for hh in two]
        if mode == "fox":
            t = [s[hh] - ct_ref[pl.ds(2 * hp + hh, 1), pl.ds(off, nblk * T)] for hh in two]
            if diagonal:
                t = [jnp.where(causal, t[hh], NEG) for hh in two]
            m_new = [jnp.maximum(m[hh], jnp.max(t[hh], axis=1, keepdims=True) + cq[hh]) for hh in two]
            p = [jnp.exp2(t[hh] - (m_new[hh] - cq[hh])) for hh in two]
        else:
            blocks = range(nblk)
            base = [[slope[hh] * ((i - (j0 + b)) * T).astype(F32) for b in blocks] for hh in two]
            t = [[s[hh][:, b * T:(b + 1) * T] - sd0[hh] for b in blocks] for hh in two]
            if diagonal:
                t = [[jnp.where(causal, t[hh][b], NEG) for b in blocks] for hh in two]
            cand = [[jnp.max(t[hh][b], axis=1, keepdims=True) - base[hh][b] for b in blocks] for hh in two]
            shift = base
            if not diagonal:
                picked = [[jnp.sum(jnp.where(blane == j0 + b, sel[hh], 0.0), axis=1, keepdims=True) > 0.5
                           for b in blocks] for hh in two]
                cand = [[jnp.where(picked[hh][b], cand[hh][b], NEG) for b in blocks] for hh in two]
                shift = [[jnp.where(picked[hh][b], base[hh][b], -NEG) for b in blocks] for hh in two]
            m_new = [functools.reduce(jnp.maximum, cand[hh], m[hh]) for hh in two]
            ps = [[jnp.exp2(t[hh][b] - (m_new[hh] + shift[hh][b])) for b in blocks] for hh in two]
            p = [ps[hh][0] if nblk == 1 else jnp.concatenate(ps[hh], axis=1) for hh in two]
        alpha = [jnp.exp2(m[hh] - m_new[hh]) for hh in two]
        l = [alpha[hh] * carry[hh][1] + jnp.sum(p[hh], axis=1, keepdims=True) for hh in two]
        pv = [jnp.dot(p[hh].astype(BF16), v2, preferred_element_type=F32) for hh in two]
        return tuple((m_new[hh], l[hh], alpha[hh] * carry[hh][2] + pv[hh]) for hh in two)

    init = tuple((jnp.full((T, 1), NEG, F32), jnp.zeros((T, 1), F32), jnp.zeros((T, LANES), F32)) for _ in range(2))
    carry = chunk(i, 1, init, True)
    n_wide = i // WIDE
    carry = lax.fori_loop(0, n_wide, lambda jw, cr: chunk(jw * WIDE, WIDE, cr, False), carry)
    carry = lax.fori_loop(n_wide * WIDE, i, lambda j, cr: chunk(j, 1, cr, False), carry)
    (_, l_a, acc_a), (_, l_b, acc_b) = carry
    o_ref[...] = jnp.where(lo, acc_a / l_a, acc_b / l_b)


def _prompt_attn(mode, q, kb, vb, n_rows, *extra):
    nq = n_rows // ATT_TILE
    qspec = pl.BlockSpec((ATT_TILE, LANES), lambda hp, i, *_: (i, hp))
    kvspec = pl.BlockSpec((n_rows, LANES), lambda hp, i, *_: (0, hp))
    if mode == "fox":
        c, ct = extra
        in_specs = [qspec, kvspec, kvspec, pl.BlockSpec((ATT_TILE, N_HEADS), lambda hp, i: (i, 0)),
                    _const_spec((N_HEADS, n_rows))]
        args = (q, kb, vb, c, ct)
        n_prefetch = 0
    else:
        slopes, kmean = extra
        in_specs = [qspec, kvspec, kvspec, pl.BlockSpec((kmean.shape[0], LANES), lambda hp, i, *_: (0, hp))]
        args = (slopes, q, kb, vb, kmean)
        n_prefetch = 1
    return pl.pallas_call(
        functools.partial(_prompt_attn_kernel, mode=mode),
        grid_spec=pltpu.PrefetchScalarGridSpec(
            num_scalar_prefetch=n_prefetch, grid=(N_HEADS // 2, nq), in_specs=in_specs, out_specs=qspec),
        out_shape=jax.ShapeDtypeStruct((n_rows, WIDTH), F32),
        compiler_params=_cparams("arbitrary", "arbitrary"),
        name=mode + "_prompt",
    )(*args)


def _head_rows(x8):
    return jnp.concatenate([jnp.broadcast_to(x8[h:h + 1, :], (8, x8.shape[1])) for h in range(N_HEADS)], axis=0)


def _block_diag_q(q, scale):
    lane = lax.broadcasted_iota(jnp.int32, (1, WIDTH), 1)
    return jnp.concatenate([jnp.where(lane // HEAD_DIM == h, q * scale, 0.0) for h in range(N_HEADS)], axis=0)


def _pages_t(cache):
    d, n = cache.shape[:2]
    return jnp.transpose(cache, (0, 1, 3, 4, 2)).reshape(d, n, WIDTH, PAGE)


def _page_spec(layer, n_pages, i, reverse):
    def index(b, p, pt, *_):
        pg = p * PAGES_PER_STEP + i
        if reverse:
            pg = n_pages - 1 - pg
        return (layer, pt[b, pg], 0, 0)
    return pl.BlockSpec((None, None, WIDTH, PAGE), index)


def _sample_select_kernel(pt_ref, q_ref, knew_ref, *refs, n_pages):
    P = PAGES_PER_STEP
    k_refs, sel_ref, kmt_ref = refs[:P], refs[P], refs[P + 1]
    p = pl.program_id(1)
    per_blk = MOBA_BLOCK // PAGE
    lane = lax.broadcasted_iota(jnp.int32, (1, LANES), 1)

    @pl.when(p == 0)
    def _():
        kmt_ref[...] = jnp.zeros(kmt_ref.shape, F32)

    kmt = kmt_ref[...]
    for i in range(P // per_blk):
        tot = sum(k_refs[per_blk * i + t][...] for t in range(per_blk))
        col = jnp.sum(tot, axis=1, keepdims=True) * (1.0 / MOBA_BLOCK)
        kmt = jnp.where(lane == p * (P // per_blk) + i, col, kmt)
    kmt_ref[...] = kmt

    @pl.when(p == pl.num_programs(1) - 1)
    def _():
        n_past = n_pages // per_blk
        qbd = _block_diag_q(q_ref[...], 1.0)
        bs = _dot_hi(qbd, kmt_ref[...])
        own_mean = jnp.sum(knew_ref[...], axis=0, keepdims=True) * (1.0 / MOBA_BLOCK)
        bs = jnp.where(lane == n_past, jnp.sum(qbd * own_mean, axis=1, keepdims=True), bs)
        sel_ref[...] = _top3(bs, lane < n_past, n_past)


def _sample_select(page_table, q, knew, cache_kt, layer, row0):
    n_seq, n_pages = page_table.shape
    t = knew.shape[0] // n_seq
    P = PAGES_PER_STEP
    return pl.pallas_call(
        functools.partial(_sample_select_kernel, n_pages=n_pages),
        grid_spec=pltpu.PrefetchScalarGridSpec(
            num_scalar_prefetch=1, grid=(n_seq, n_pages // P),
            in_specs=[pl.BlockSpec((t, WIDTH), lambda b, p, pt: (row0 // t + b, 0)),
                      pl.BlockSpec((t, WIDTH), lambda b, p, pt: (b, 0))]
            + [_page_spec(layer, n_pages, i, False) for i in range(P)],
            out_specs=pl.BlockSpec((None, N_HEADS * t, LANES), lambda b, p, pt: (b, 0, 0)),
            scratch_shapes=[pltpu.VMEM((WIDTH, LANES), F32)]),
        out_shape=jax.ShapeDtypeStruct((n_seq, N_HEADS * t, LANES), F32),
        compiler_params=_cparams("arbitrary", "arbitrary"),
        name="moba_sample_select",
    )(page_table, q, knew, *([cache_kt] * P))


def _sample_attn_kernel(*refs, mode, n_pages):
    P = PAGES_PER_STEP
    if mode == "fox":
        pt_ref, q_ref, knew_ref, vnew_ref, lnew_ref = refs[:5]
        rest = refs[5:]
        lf_refs, rest = rest[:P], rest[P:]
    else:
        pt_ref, slopes_ref, q_ref, knew_ref, vnew_ref, sel_ref = refs[:6]
        rest = refs[6:]
    k_refs, v_refs, rest = rest[:P], rest[P:2 * P], rest[2 * P:]
    o_ref, qbd_ref, m_ref, l_ref, acc_ref, carry_ref, cq_ref = rest
    p = pl.program_id(1)
    R = qbd_ref.shape[0]
    t_new = R // N_HEADS
    lane = lax.broadcasted_iota(jnp.int32, (1, LANES), 1)
    trow = lax.broadcasted_iota(jnp.int32, (R, 1), 0) % t_new
    if mode == "moba":
        srow = jnp.concatenate([jnp.full((t_new, 1), slopes_ref[h], F32) for h in range(N_HEADS)], axis=0)

    @pl.when(p == 0)
    def _():
        qbd_ref[...] = _block_diag_q(q_ref[...], ATTN_SCALE).astype(BF16)
        pad = jnp.zeros((PAGE - t_new, WIDTH), F32)
        s = _dot_nt(qbd_ref[...], jnp.concatenate([knew_ref[...], pad], axis=0))
        if mode == "fox":
            x = lnew_ref[...]
            d = 1
            while d < t_new:
                x = x + jnp.where(lane >= d, pltpu.roll(x, d, 1), 0.0)
                d *= 2
            crep = _head_rows(x)
            cq = jnp.sum(jnp.where(lane == trow, crep, 0.0), axis=1, keepdims=True)
            cq_ref[...] = cq
            carry_ref[...] = jnp.zeros(carry_ref.shape, F32)
            s = s + cq - crep
        else:
            s = s - srow * (trow - lane).astype(F32)
        s = jnp.where(lane <= trow, s, NEG)
        m = jnp.max(s, axis=1, keepdims=True)
        pr = jnp.exp(s - m)
        m_ref[...] = m
        l_ref[...] = jnp.sum(pr, axis=1, keepdims=True)
        acc_ref[...] = _dot(pr, jnp.concatenate([vnew_ref[...], pad], axis=0))

    qbd = qbd_ref[...]
    parts = []
    if mode == "fox":
        carry = carry_ref[...]
    for i in range(P):
        pg = n_pages - 1 - (p * P + i)
        s = _dot(qbd, k_refs[i][...])
        if mode == "fox":
            lf = lf_refs[i][...]
            x = lf
            d = 1
            while d < PAGE:
                x = x + jnp.where(lane < PAGE - d, pltpu.roll(x, PAGE - d, 1), 0.0)
                d *= 2
            bias = (x - lf) + carry
            carry = carry + x[:, 0:1]
            s = s + _head_rows(bias) + cq_ref[...]
        else:
            dist = (n_pages * PAGE + trow) - (pg * PAGE + lane)
            s = s - srow * dist.astype(F32)
            picked = jnp.sum(jnp.where(lane == pg // (MOBA_BLOCK // PAGE), sel_ref[...], 0.0), axis=1, keepdims=True) > 0.5
            s = jnp.where(picked, s, NEG)
        parts.append(s)
    if mode == "fox":
        carry_ref[...] = carry
    s_all = jnp.concatenate(parts, axis=1)
    m_old = m_ref[...]
    m_new = jnp.maximum(m_old, jnp.max(s_all, axis=1, keepdims=True))
    alpha = jnp.exp(m_old - m_new)
    pr = jnp.exp(s_all - m_new)
    m_ref[...] = m_new
    l_ref[...] = alpha * l_ref[...] + jnp.sum(pr, axis=1, keepdims=True)
    pv = sum(_dot_nt(pr[:, i * PAGE:(i + 1) * PAGE], v_refs[i][...]) for i in range(P))
    acc_ref[...] = alpha * acc_ref[...] + pv

    @pl.when(p == pl.num_programs(1) - 1)
    def _():
        res = acc_ref[...] / l_ref[...]
        wl = lax.broadcasted_iota(jnp.int32, (1, WIDTH), 1)
        out = jnp.zeros((t_new, WIDTH), F32)
        for h in range(N_HEADS):
            out = jnp.where(wl // HEAD_DIM == h, res[h * t_new:(h + 1) * t_new, :], out)
        o_ref[...] = out


def _sample_attn(mode, page_table, q, knew, vnew, cache_kt, cache_vt, layer, row0, *extra):
    n_seq, n_pages = page_table.shape
    t = knew.shape[0] // n_seq
    R = N_HEADS * t
    P = PAGES_PER_STEP
    nsp = 1 if mode == "fox" else 2
    qtok = pl.BlockSpec((t, WIDTH), lambda b, p, *_: (row0 // t + b, 0))
    tok = pl.BlockSpec((t, WIDTH), lambda b, p, *_: (b, 0))
    pages = [_page_spec(layer, n_pages, i, True) for i in range(P)]
    if mode == "fox":
        lnew_t, lf_t = extra
        def lf_spec(i):
            return pl.BlockSpec((None, N_HEADS, PAGE), lambda b, p, pt: (pt[b, n_pages - 1 - (p * P + i)], 0, 0))
        in_specs = ([qtok, tok, tok, pl.BlockSpec((None, N_HEADS, LANES), lambda b, p, pt: (b, 0, 0))]
                    + [lf_spec(i) for i in range(P)] + pages + pages)
        args = (page_table, q, knew, vnew, lnew_t, *([lf_t] * P), *([cache_kt] * P), *([cache_vt] * P))
    else:
        slopes, sel = extra
        in_specs = [qtok, tok, tok, pl.BlockSpec((None, R, LANES), lambda b, p, pt, sl: (b, 0, 0))] + pages + pages
        args = (page_table, slopes, q, knew, vnew, sel, *([cache_kt] * P), *([cache_vt] * P))
    return pl.pallas_call(
        functools.partial(_sample_attn_kernel, mode=mode, n_pages=n_pages),
        grid_spec=pltpu.PrefetchScalarGridSpec(
            num_scalar_prefetch=nsp, grid=(n_seq, n_pages // P), in_specs=in_specs, out_specs=tok,
            scratch_shapes=[pltpu.VMEM((R, WIDTH), BF16), pltpu.VMEM((R, 1), F32), pltpu.VMEM((R, 1), F32),
                            pltpu.VMEM((R, WIDTH), F32), pltpu.VMEM((N_HEADS, 1), F32), pltpu.VMEM((R, 1), F32)]),
        out_shape=jax.ShapeDtypeStruct((n_seq * t, WIDTH), F32),
        compiler_params=_cparams("arbitrary", "arbitrary"),
        name=mode + "_sample",
    )(*args)


def _merge_kernel(h_ref, gate_ref, orw_ref, ofox_ref, omoba_ref, wr_ref, wf_ref, wm_ref, wo_ref, g_ref, o_ref):
    m = (gate_ref[:, 0:D_MODEL] * _dot(orw_ref[...], wr_ref[...])
         + gate_ref[:, D_MODEL:2 * D_MODEL] * _dot(ofox_ref[...], wf_ref[...])
         + gate_ref[:, 2 * D_MODEL:3 * D_MODEL] * _dot(omoba_ref[...], wm_ref[...]))
    o_ref[...] = h_ref[...] + _rms(_dot(m, wo_ref[...]), g_ref[...])


def _merge(h, gates, o_rw, o_fox, o_moba, wr, wf, wm, wo, g):
    m = h.shape[0]

    def row(n):
        return pl.BlockSpec((ROW_TILE, n), lambda i: (i, 0))

    wspec = _const_spec((WIDTH, D_MODEL))
    return pl.pallas_call(
        _merge_kernel,
        grid=(m // ROW_TILE,),
        in_specs=[row(D_MODEL), row(3 * D_MODEL), row(WIDTH), row(WIDTH), row(WIDTH), wspec, wspec, wspec,
                  _const_spec((D_MODEL, D_MODEL)), _const_spec((1, D_MODEL))],
        out_specs=row(D_MODEL),
        out_shape=jax.ShapeDtypeStruct((m, D_MODEL), F32),
        compiler_params=_cparams("arbitrary"),
        name="merge",
    )(h, gates, o_rw, o_fox, o_moba, wr, wf, wm, wo, g)


def _pack_w_in(w_in, b_forget):
    o = 0
    def take(n):
        nonlocal o
        s = w_in[:, o:o + n]
        o += n
        return s
    rkv = take(3 * WIDTH)
    wd, ad, gd = take(W_LORA), take(A_LORA), take(G_LORA)
    fox = take(3 * WIDTH)
    fl = take(N_HEADS)
    moba = take(3 * WIDTH)
    gates = take(3 * D_MODEL)
    z = lambda n: jnp.zeros((D_MODEL, n), w_in.dtype)
    packed = jnp.concatenate([rkv, wd, z(LANES - W_LORA), ad, z(LANES - A_LORA), gd, fox, moba, gates,
                              fl, z(LANES - N_HEADS)], axis=1)
    bf = jnp.concatenate([b_forget, jnp.zeros((LANES - N_HEADS,), F32)])[None, :]
    return packed.astype(BF16), bf


def _pad_rows(w, n):
    return jnp.concatenate([w, jnp.zeros((n - w.shape[0], w.shape[1]), w.dtype)], axis=0)


def _pack_mu(mu):
    z = jnp.zeros((LANES - W_LORA,), F32)
    return jnp.concatenate([mu[:3 * WIDTH], mu[3 * WIDTH:3 * WIDTH + W_LORA], z,
                            mu[3 * WIDTH + W_LORA:3 * WIDTH + W_LORA + A_LORA], z,
                            mu[3 * WIDTH + W_LORA + A_LORA:]])[None, :]


def _block_diag_state(s):
    b = s.shape[0]
    eye = jnp.eye(N_HEADS, dtype=s.dtype)
    return jnp.einsum('bhij,hg->bhigj', s, eye).reshape(b, WIDTH, WIDTH)


def _unblock_state(s):
    b = s.shape[0]
    s5 = s.reshape(b, N_HEADS, HEAD_DIM, N_HEADS, HEAD_DIM)
    return jnp.stack([s5[:, h, :, h, :] for h in range(N_HEADS)], axis=1)


def kernel(x_prompt, x_sample, cache_fox_k, cache_fox_v, cache_fox_logf, cache_moba_k, cache_moba_v, state_rwkv_wkv, state_rwkv_shift, page_table, norms, w_in, b_forget, rwkv_mu, rwkv_w0, rwkv_w_up, rwkv_a0, rwkv_a_up, rwkv_g_up, rwkv_k_k, rwkv_k_a, rwkv_r_k, rwkv_ln_w, rwkv_ln_b, w_proj_rwkv, w_proj_fox, w_proj_moba, w_out, ffn_gate, ffn_up, ffn_down):
    depth = norms.shape[0]
    n_p = x_prompt.shape[0] * x_prompt.shape[1]
    n_seq, t_s = x_sample.shape[0], x_sample.shape[1]
    n_s = n_seq * t_s
    assert x_prompt.shape[0] == 1 and n_p % ROW_TILE == 0 and n_s % ROW_TILE == 0 and n_p % (8 * MOBA_BLOCK) == 0

    x = jnp.concatenate([x_prompt.reshape(n_p, D_MODEL), x_sample.reshape(n_s, D_MODEL)], axis=0)
    slopes = 2.0 ** (-8.0 * jnp.arange(1, N_HEADS + 1, dtype=F32) / N_HEADS)
    hid = jnp.arange(WIDTH) // HEAD_DIM
    bd = (hid[:, None] == hid[None, :]).astype(BF16)
    ar = jnp.arange(LANES)
    tri = (ar[:, None] <= ar[None, :]).astype(BF16)
    n_rows_c = N_HEADS * n_p // LANES
    rr = jnp.arange(n_rows_c)
    blk = ((rr[:, None] // (n_p // LANES) == rr[None, :] // (n_p // LANES)) & (rr[None, :] < rr[:, None])).astype(BF16)
    cfk, cfv, cmk, cmv = (_pages_t(c) for c in (cache_fox_k, cache_fox_v, cache_moba_k, cache_moba_v))
    vec = lambda a: a.reshape(1, -1)

    outs_p, outs_s = [], []
    for l in range(depth):
        bfc = lambda a: a.astype(BF16)
        h = _ffn(x, vec(norms[l, 0]), vec(norms[l, 1]), bfc(ffn_gate[l, 0]), bfc(ffn_up[l, 0]), bfc(ffn_down[l, 0]))
        w_pack, bf_pad = _pack_w_in(w_in[l], b_forget[l])
        (u, rw, fq, fk, fv, fkb, fvb, logf, mq, mk, mv, mkb, mvb, gates) = _inproj(h, vec(norms[l, 2]), w_pack, bf_pad)

        prev_u = jnp.concatenate([jnp.zeros((8, D_MODEL), F32), state_rwkv_shift[l]], axis=0)
        prev_rw = _mm(prev_u, w_pack[:, :RW_PACK])
        rparams = (_pack_mu(rwkv_mu[l]), vec(rwkv_w0[l]), bfc(_pad_rows(rwkv_w_up[l], LANES)), vec(rwkv_a0[l]),
                   bfc(_pad_rows(rwkv_a_up[l], LANES)), bfc(rwkv_g_up[l]), vec(rwkv_k_k[l]), vec(rwkv_k_a[l]),
                   vec(rwkv_r_k[l]), vec(rwkv_ln_w[l]), vec(rwkv_ln_b[l]))
        o_rw_p, s_p = _rwkv(rw, prev_rw[0:1][None], jnp.zeros((1, WIDTH, WIDTH), F32), rparams, bd,
                            n_seq=1, t_seq=n_p, row0=0)
        o_rw_s, s_s = _rwkv(rw, prev_rw[8:8 + n_seq][:, None, :], _block_diag_state(state_rwkv_wkv[l]), rparams, bd,
                            n_seq=n_seq, t_seq=t_s, row0=n_p)

        lf_t = logf[:n_p, :N_HEADS].T
        ct = _cumsum_rows(lf_t.reshape(n_rows_c, LANES), tri, blk).reshape(N_HEADS, n_p)
        o_fox_p = _prompt_attn("fox", fq, fkb, fvb, n_p, ct.T, ct)
        lnew_t = jnp.swapaxes(logf[n_p:, :N_HEADS].reshape(n_seq, t_s, N_HEADS), 1, 2)
        lnew_t = jnp.concatenate([lnew_t, jnp.zeros((n_seq, N_HEADS, LANES - t_s), F32)], axis=2)
        lf_pool_t = jnp.swapaxes(cache_fox_logf[l].astype(F32), 1, 2)
        o_fox_s = _sample_attn("fox", page_table, fq, fk[n_p:], fv[n_p:], cfk, cfv, l, n_p, lnew_t, lf_pool_t)

        kmean = _kmean(mk, n_p)
        o_moba_p = _prompt_attn("moba", mq, mkb, mvb, n_p, slopes, kmean)
        sel = _sample_select(page_table, mq, mk[n_p:], cmk, l, n_p)
        o_moba_s = _sample_attn("moba", page_table, mq, mk[n_p:], mv[n_p:], cmk, cmv, l, n_p, slopes, sel)

        cat = lambda a, b: jnp.concatenate([a, b], axis=0)
        h = _merge(h, gates, cat(o_rw_p, o_rw_s), cat(o_fox_p, o_fox_s), cat(o_moba_p, o_moba_s),
                   bfc(w_proj_rwkv[l]), bfc(w_proj_fox[l]), bfc(w_proj_moba[l]), bfc(w_out[l]), vec(norms[l, 3]))
        x = _ffn(h, vec(norms[l, 4]), vec(norms[l, 5]), bfc(ffn_gate[l, 1]), bfc(ffn_up[l, 1]), bfc(ffn_down[l, 1]))

        hd = lambda a, b, t: a.reshape(b, t, N_HEADS, HEAD_DIM)
        outs_p.append((hd(fk[:n_p], 1, n_p), hd(fv[:n_p], 1, n_p), logf[:n_p, :N_HEADS].reshape(1, n_p, N_HEADS),
                       hd(mk[:n_p], 1, n_p), hd(mv[:n_p], 1, n_p), _unblock_state(s_p), u[n_p - 1:n_p]))
        outs_s.append((hd(fk[n_p:], n_seq, t_s), hd(fv[n_p:], n_seq, t_s),
                       logf[n_p:, :N_HEADS].reshape(n_seq, t_s, N_HEADS), hd(mk[n_p:], n_seq, t_s),
                       hd(mv[n_p:], n_seq, t_s), _unblock_state(s_s),
                       u[n_p:].reshape(n_seq, t_s, D_MODEL)[:, -1]))

    st = lambda rows, i: jnp.stack([r[i] for r in rows])
    return (x[:n_p].reshape(x_prompt.shape), x[n_p:].reshape(x_sample.shape),
            *[st(outs_p, i) for i in range(7)], *[st(outs_s, i) for i in range(7)])
```

```python
import functools

import jax
import jax.numpy as jnp
from jax import lax
from jax.experimental import pallas as pl
from jax.experimental.pallas import tpu as pltpu

F32 = jnp.float32
BF16 = jnp.bfloat16

D_MODEL = 1024
HEAD_DIM = 64
N_HEADS = 8
WIDTH = N_HEADS * HEAD_DIM
W_LORA, A_LORA, G_LORA = 64, 64, 128
D_FF = 2816
PAGE = 128
MOBA_BLOCK = 256
MOBA_TOPK = 3
RMS_EPS = 1e-6
GN_EPS = 64e-5
ATTN_SCALE = HEAD_DIM ** -0.5
NEG = -1e30
LANES = 128
RW_PACK = 3 * WIDTH + 3 * LANES
IN_PACK = RW_PACK + 6 * WIDTH + 3 * D_MODEL + LANES
ROW_TILE = 256
RW_CHUNK = 128
RW_CHUNK_MIN = 32
ATT_TILE = 256
WIDE = (8, 4)
PAGES_PER_STEP = 16
LOG2E = 1.4426950408889634
VMEM_LIMIT = 56 * 1024 * 1024


def _cparams(*sem):
    return pltpu.CompilerParams(dimension_semantics=sem, vmem_limit_bytes=VMEM_LIMIT)


def _dot(a, b):
    return jnp.dot(a.astype(BF16), b.astype(BF16), preferred_element_type=F32)


def _dot_nt(a, b):
    return lax.dot_general(a.astype(BF16), b.astype(BF16), (((1,), (1,)), ((), ())), preferred_element_type=F32)


def _dot_tn(a, b):
    return lax.dot_general(a.astype(BF16), b.astype(BF16), (((0,), (0,)), ((), ())), preferred_element_type=F32)


def _split3(a):
    a1 = a.astype(BF16)
    r1 = a - a1.astype(F32)
    a2 = r1.astype(BF16)
    a3 = (r1 - a2.astype(F32)).astype(BF16)
    return a1, a2, a3


def _dot_exact_rhs(a, b01):
    a1, a2, a3 = _split3(a)
    return _dot(a1, b01) + _dot(a2, b01) + _dot(a3, b01)


def _dot_exact_lhs(a01, b):
    b1, b2, b3 = _split3(b)
    return _dot(a01, b1) + _dot(a01, b2) + _dot(a01, b3)


def _dot_nt_hi(a, b):
    a1, a2, a3 = _split3(a)
    b1, b2, b3 = _split3(b)
    return (_dot_nt(a1, b1) + _dot_nt(a1, b2) + _dot_nt(a2, b1)
            + _dot_nt(a1, b3) + _dot_nt(a2, b2) + _dot_nt(a3, b1))


def _dot_hi(a, b):
    a1, a2, a3 = _split3(a)
    b1, b2, b3 = _split3(b)
    return _dot(a1, b1) + _dot(a1, b2) + _dot(a2, b1) + _dot(a1, b3) + _dot(a2, b2) + _dot(a3, b1)


def _rms(x, g):
    return x * lax.rsqrt(jnp.mean(x * x, axis=-1, keepdims=True) + RMS_EPS) * g


def _sigmoid(x):
    return 1.0 / (1.0 + jnp.exp(-x))


def _softplus(x):
    return jnp.maximum(x, 0.0) + jnp.log1p(jnp.exp(-jnp.abs(x)))


def _const_spec(shape):
    nd = len(shape)
    return pl.BlockSpec(shape, lambda *_: (0,) * nd, pipeline_mode=pl.Buffered(1))


def _ffn_kernel(x_ref, gpre_ref, gpost_ref, wg_ref, wu_ref, wd_ref, o_ref):
    x = x_ref[...]
    xn = _rms(x, gpre_ref[...]).astype(BF16)
    g = jnp.dot(xn, wg_ref[...], preferred_element_type=F32)
    u = jnp.dot(xn, wu_ref[...], preferred_element_type=F32)
    a = (g * _sigmoid(g) * u).astype(BF16)
    y = jnp.dot(a, wd_ref[...], preferred_element_type=F32)
    o_ref[...] = x + 0.5 * _rms(y, gpost_ref[...])


def _ffn(x, gpre, gpost, wg, wu, wd):
    m = x.shape[0]
    row = pl.BlockSpec((ROW_TILE, D_MODEL), lambda i: (i, 0))
    return pl.pallas_call(
        _ffn_kernel,
        grid=(m // ROW_TILE,),
        in_specs=[row, _const_spec((1, D_MODEL)), _const_spec((1, D_MODEL)),
                  _const_spec((D_MODEL, D_FF)), _const_spec((D_MODEL, D_FF)), _const_spec((D_FF, D_MODEL))],
        out_specs=row,
        out_shape=jax.ShapeDtypeStruct((m, D_MODEL), F32),
        compiler_params=_cparams("arbitrary"),
        name="ffn",
    )(x, gpre, gpost, wg, wu, wd)


def _inproj_kernel(h_ref, g_ref, w_ref, bf_ref, u_ref, rw_ref, fq_ref, fk_ref, fv_ref, fkb_ref, fvb_ref,
                   lf_ref, mq_ref, mk_ref, mv_ref, mkb_ref, mvb_ref, gate_ref):
    u = _rms(h_ref[...], g_ref[...])
    u_ref[...] = u
    ub = u.astype(BF16)

    def proj(lo, n):
        return jnp.dot(ub, w_ref[:, lo:lo + n], preferred_element_type=F32)

    rw_ref[...] = proj(0, RW_PACK)
    o = RW_PACK
    fq_ref[...] = proj(o, WIDTH)
    fk = proj(o + WIDTH, WIDTH)
    fv = proj(o + 2 * WIDTH, WIDTH)
    fk_ref[...] = fk
    fv_ref[...] = fv
    fkb_ref[...] = fk.astype(BF16)
    fvb_ref[...] = fv.astype(BF16)
    o += 3 * WIDTH
    mq_ref[...] = proj(o, WIDTH)
    mk = proj(o + WIDTH, WIDTH)
    mv = proj(o + 2 * WIDTH, WIDTH)
    mk_ref[...] = mk
    mv_ref[...] = mv
    mkb_ref[...] = mk.astype(BF16)
    mvb_ref[...] = mv.astype(BF16)
    o += 3 * WIDTH
    gate_ref[...] = _sigmoid(proj(o, 3 * D_MODEL))
    o += 3 * D_MODEL
    lf_ref[...] = -_softplus(-(proj(o, LANES) + bf_ref[...]))


def _inproj(h, g, w_pack, bf_pad):
    m = h.shape[0]

    def row(n):
        return pl.BlockSpec((ROW_TILE, n), lambda i: (i, 0))

    widths = [(D_MODEL, F32), (RW_PACK, F32), (WIDTH, F32), (WIDTH, F32), (WIDTH, F32), (WIDTH, BF16), (WIDTH, BF16),
              (LANES, F32), (WIDTH, F32), (WIDTH, F32), (WIDTH, F32), (WIDTH, BF16), (WIDTH, BF16), (3 * D_MODEL, F32)]
    return pl.pallas_call(
        _inproj_kernel,
        grid=(m // ROW_TILE,),
        in_specs=[row(D_MODEL), _const_spec((1, D_MODEL)), _const_spec((D_MODEL, IN_PACK)), _const_spec((1, LANES))],
        out_specs=[row(n) for n, _ in widths],
        out_shape=[jax.ShapeDtypeStruct((m, n), dt) for n, dt in widths],
        compiler_params=_cparams("arbitrary"),
        name="inproj",
    )(h, g, w_pack, bf_pad)


def _mm_kernel(a_ref, b_ref, o_ref):
    o_ref[...] = jnp.dot(a_ref[...].astype(BF16), b_ref[...], preferred_element_type=F32)


def _mm(a, b):
    return pl.pallas_call(
        _mm_kernel,
        out_shape=jax.ShapeDtypeStruct((a.shape[0], b.shape[1]), F32),
        compiler_params=pltpu.CompilerParams(vmem_limit_bytes=VMEM_LIMIT),
        name="shift_proj",
    )(a, b)


def _rwkv_kernel(rw_ref, prev_ref, s0_ref, mu_ref, w0_ref, wup_ref, a0_ref, aup_ref, gup_ref, kk_ref, ka_ref,
                 rk_ref, lnw_ref, lnb_ref, bd_ref, o_ref, sout_ref, last_ref, s_ref, *, t_real, C):
    c = pl.program_id(1)

    @pl.when(c == 0)
    def _():
        last_ref[...] = prev_ref[...]
        s_ref[...] = s0_ref[...]

    rw = rw_ref[...]
    if t_real < C:
        rw = jnp.concatenate([rw, jnp.zeros((C - t_real, RW_PACK), F32)], axis=0)
    row = lax.broadcasted_iota(jnp.int32, (C, 1), 0)
    rw_prev = jnp.where(row == 0, last_ref[...], pltpu.roll(rw, 1, 0))
    last_ref[...] = rw[C - 1:C, :]
    xm = rw + (rw_prev - rw) * mu_ref[...]
    if t_real < C:
        xm = jnp.where(row < t_real, xm, 0.0)
    r = xm[:, 0:WIDTH]
    k = xm[:, WIDTH:2 * WIDTH]
    v = xm[:, 2 * WIDTH:3 * WIDTH]
    wd = xm[:, 3 * WIDTH:3 * WIDTH + LANES]
    ad = xm[:, 3 * WIDTH + LANES:3 * WIDTH + 2 * LANES]
    gd = xm[:, 3 * WIDTH + 2 * LANES:RW_PACK]
    bd = bd_ref[...]

    z = w0_ref[...] + _dot(jnp.tanh(wd), wup_ref[...])
    lw = -jnp.exp(-_softplus(-z) - 0.5)
    if t_real < C:
        lw = jnp.where(row < t_real, lw, 0.0)
    ag = _sigmoid(a0_ref[...] + _dot(ad, aup_ref[...]))
    g = _dot(_sigmoid(gd), gup_ref[...])
    kk = k * kk_ref[...]
    kk = kk / jnp.maximum(jnp.sqrt(_dot_exact_rhs(kk * kk, bd)), 1e-12)
    kh = k * (1.0 + (ag - 1.0) * ka_ref[...])
    a_ = -kk
    b_ = kk * ag

    ti = lax.broadcasted_iota(jnp.int32, (C, C), 0)
    si = lax.broadcasted_iota(jnp.int32, (C, C), 1)
    incl = si <= ti
    strict = si < ti
    cum = _dot_exact_lhs(incl.astype(BF16), lw)
    tot = cum[C - 1:C, :]
    p_end = jnp.exp(tot - cum)
    p_inv = jnp.exp(-cum)
    at = a_ * jnp.exp(cum - lw)
    rt = r * jnp.exp(cum)
    bt = (b_ * p_inv).astype(BF16)
    kt = (kh * p_inv).astype(BF16)
    bh = b_ * p_end
    khat = kh * p_end
    vb = v.astype(BF16)
    eye = (si == ti).astype(F32)
    lane = lax.broadcasted_iota(jnp.int32, (1, LANES), 1)

    heads = range(N_HEADS)
    pair = lambda x, h: x[:, (h // 2) * LANES:(h // 2 + 1) * LANES]
    mh = [(lane < HEAD_DIM) if h % 2 == 0 else (lane >= HEAD_DIM) for h in heads]
    at_h = [jnp.where(mh[h], pair(at, h), 0.0).astype(BF16) for h in heads]
    rt_h = [jnp.where(mh[h], pair(rt, h), 0.0).astype(BF16) for h in heads]
    a_ab = [jnp.where(strict, _dot_nt(at_h[h], pair(bt, h)), 0.0) for h in heads]
    a_ak = [jnp.where(strict, _dot_nt(at_h[h], pair(kt, h)), 0.0) for h in heads]
    a_rb = [jnp.where(incl, _dot_nt(rt_h[h], pair(bt, h)), 0.0) for h in heads]
    a_rk = [jnp.where(incl, _dot_nt(rt_h[h], pair(kt, h)), 0.0) for h in heads]
    tm = [eye + a_ab[h] for h in heads]
    apow = a_ab
    n = 2
    while n < C:
        apow = [_dot(apow[h], apow[h]) for h in heads]
        tm = [tm[h] + _dot(tm[h], apow[h]) for h in heads]
        n *= 2
    ua = [_dot(tm[h], pair(at, h)) for h in heads]
    akv = [_dot(a_ak[h], pair(vb, h)) for h in heads]
    uv = [_dot(tm[h], akv[h]) for h in heads]
    qy = [pair(rt, h) + _dot(a_rb[h], ua[h]) for h in heads]
    yc = [_dot(a_rb[h], uv[h]) + _dot(a_rk[h], pair(vb, h)) for h in heads]
    merge = lambda xs: jnp.concatenate([jnp.where(mh[0], xs[h], xs[h + 1]) for h in range(0, N_HEADS, 2)], axis=1)
    ua_all, uv_all, qy_all, yc_all = merge(ua), merge(uv), merge(qy), merge(yc)

    s0 = s_ref[...]
    y = _dot_nt(qy_all, s0) + yc_all
    bdm = bd > 0
    gc = jnp.where(bdm, _dot_tn(ua_all, bh), 0.0)
    hm = jnp.where(bdm, _dot_tn(uv_all, bh) + _dot_tn(v, khat), 0.0)
    s_new = s0 * jnp.exp(tot) + _dot(s0, gc) + hm
    s_ref[...] = s_new
    sout_ref[...] = s_new

    inv_n = 1.0 / HEAD_DIM
    mean = _dot_exact_rhs(y, bd) * inv_n
    yd = y - mean
    var = _dot_exact_rhs(yd * yd, bd) * inv_n
    yn = yd * lax.rsqrt(var + GN_EPS) * lnw_ref[...] + lnb_ref[...]
    bonus = _dot_exact_rhs(r * kh * rk_ref[...], bd) * v
    o = (yn + bonus) * g
    o_ref[...] = o[:t_real] if t_real < C else o


def _rwkv(rw, prev, s0bd, params, bd, *, n_seq, t_seq, row0):
    t_real = min(t_seq, RW_CHUNK)
    n_chunk = max(t_seq // RW_CHUNK, 1)
    chunk = RW_CHUNK if t_seq >= RW_CHUNK else max(RW_CHUNK_MIN, t_seq)
    blk0 = row0 // t_real

    def vec(n):
        return _const_spec((1, n))

    in_specs = [pl.BlockSpec((t_real, RW_PACK), lambda b, c: (blk0 + b * n_chunk + c, 0)),
                pl.BlockSpec((None, 1, RW_PACK), lambda b, c: (b, 0, 0)),
                pl.BlockSpec((None, WIDTH, WIDTH), lambda b, c: (b, 0, 0)),
                vec(RW_PACK), vec(WIDTH), _const_spec((LANES, WIDTH)), vec(WIDTH), _const_spec((LANES, WIDTH)),
                _const_spec((G_LORA, WIDTH)), vec(WIDTH), vec(WIDTH), vec(WIDTH), vec(WIDTH), vec(WIDTH),
                _const_spec((WIDTH, WIDTH))]
    return pl.pallas_call(
        functools.partial(_rwkv_kernel, t_real=t_real, C=chunk),
        grid=(n_seq, n_chunk),
        in_specs=in_specs,
        out_specs=[pl.BlockSpec((t_real, WIDTH), lambda b, c: (b * n_chunk + c, 0)),
                   pl.BlockSpec((None, WIDTH, WIDTH), lambda b, c: (b, 0, 0))],
        out_shape=[jax.ShapeDtypeStruct((n_seq * t_seq, WIDTH), F32),
                   jax.ShapeDtypeStruct((n_seq, WIDTH, WIDTH), F32)],
        scratch_shapes=[pltpu.VMEM((1, RW_PACK), F32), pltpu.VMEM((WIDTH, WIDTH), F32)],
        compiler_params=_cparams("arbitrary", "arbitrary"),
        name="rwkv",
    )(rw, prev, s0bd, *params, bd)


def _cumsum_kernel(x_ref, tri_ref, blk_ref, o_ref):
    w = _dot_exact_rhs(x_ref[...], tri_ref[...])
    tot = jnp.broadcast_to(w[:, LANES - 1:LANES], w.shape)
    o_ref[...] = (w + _dot_exact_lhs(blk_ref[...], tot)) * LOG2E


def _cumsum_rows(x, tri, blk):
    return pl.pallas_call(
        _cumsum_kernel,
        out_shape=jax.ShapeDtypeStruct(x.shape, F32),
        compiler_params=pltpu.CompilerParams(vmem_limit_bytes=VMEM_LIMIT),
        name="fox_cumsum",
    )(x, tri, blk)


def _kmean_kernel(k_ref, o_ref):
    n = k_ref.shape[0] // MOBA_BLOCK
    o_ref[...] = jnp.sum(k_ref[...].reshape(n, MOBA_BLOCK, WIDTH), axis=1) * (1.0 / MOBA_BLOCK)


def _kmean(k, n_rows):
    per = 8
    nb = n_rows // MOBA_BLOCK
    return pl.pallas_call(
        _kmean_kernel,
        grid=(nb // per,),
        in_specs=[pl.BlockSpec((per * MOBA_BLOCK, WIDTH), lambda i: (i, 0))],
        out_specs=pl.BlockSpec((per, WIDTH), lambda i: (i, 0)),
        out_shape=jax.ShapeDtypeStruct((nb, WIDTH), F32),
        compiler_params=_cparams("arbitrary"),
        name="moba_kmean",
    )(k)


def _top3(bs, valid, own):
    lane = lax.broadcasted_iota(jnp.int32, bs.shape, 1).astype(F32)
    rem = jnp.where(valid, bs, NEG)
    sel = jnp.zeros(bs.shape, F32)
    for r in range(MOBA_TOPK):
        mx = jnp.max(rem, axis=1, keepdims=True)
        idx = jnp.min(jnp.where(rem == mx, lane, float(bs.shape[1])), axis=1, keepdims=True)
        pick = lane == idx
        sel = jnp.where(pick, jnp.where(own > r, 1.0, 0.0), sel)
        rem = jnp.where(pick, -jnp.inf, rem)
    return sel


def _prompt_attn_kernel(*refs, mode):
    if mode == "fox":
        q_ref, k_ref, v_ref, c_ref, ct_ref, o_ref = refs
    else:
        slopes_ref, q_ref, k_ref, v_ref, kmean_ref, o_ref = refs
    T = ATT_TILE
    hp = pl.program_id(0)
    i = pl.program_id(1)
    lane = lax.broadcasted_iota(jnp.int32, (1, LANES), 1)
    lo = lane < HEAD_DIM
    q = q_ref[...]
    qs = q * (ATTN_SCALE * LOG2E)
    qh = (jnp.where(lo, qs, 0.0).astype(BF16), jnp.where(lo, 0.0, qs).astype(BF16))
    ri = lax.broadcasted_iota(jnp.int32, (T, T), 0)
    ci = lax.broadcasted_iota(jnp.int32, (T, T), 1)
    causal = ci <= ri

    if mode == "fox":
        lane8 = lax.broadcasted_iota(jnp.int32, (1, N_HEADS), 1)
        cblk = c_ref[...]
        cq = [jnp.sum(jnp.where(lane8 == 2 * hp + hh, cblk, 0.0), axis=1, keepdims=True) for hh in range(2)]
    else:
        d0 = (ri - ci).astype(F32)
        slope = [slopes_ref[2 * hp + hh] * LOG2E for hh in range(2)]
        sd0 = [slope[hh] * d0 for hh in range(2)]
        nb = kmean_ref.shape[0]
        kmean = kmean_ref[...]
        blane = lax.broadcasted_iota(jnp.int32, (1, nb), 1)
        sel = []
        for hh in range(2):
            q_h = jnp.where(lo, q, 0.0) if hh == 0 else jnp.where(lo, 0.0, q)
            sel.append(_top3(_dot_nt_hi(q_h, kmean), blane < i, i))

    def chunk(j0, nblk, carry, diagonal):
        off = pl.multiple_of(j0 * T, T)
        k2 = k_ref[pl.ds(off, nblk * T), :]
        v2 = v_ref[pl.ds(off, nblk * T), :]
        two = range(2)
        m = [carry[hh][0] for hh in two]
        s = [_dot_nt(qh[hh], k2) for hh in two]
        if mode == "fox":
            t = [s[hh] - ct_ref[pl.ds(2 * hp + hh, 1), pl.ds(off, nblk * T)] for hh in two]
            if diagonal:
                t = [jnp.where(causal, t[hh], NEG) for hh in two]
            m_new = [jnp.maximum(m[hh], jnp.max(t[hh], axis=1, keepdims=True) + cq[hh]) for hh in two]
            p = [jnp.exp2(t[hh] - (m_new[hh] - cq[hh])) for hh in two]
        else:
            blocks = range(nblk)
            base = [[slope[hh] * ((i - (j0 + b)) * T).astype(F32) for b in blocks] for hh in two]
            t = [[s[hh][:, b * T:(b + 1) * T] - sd0[hh] for b in blocks] for hh in two]
            if diagonal:
                t = [[jnp.where(causal, t[hh][b], NEG) for b in blocks] for hh in two]
            cand = [[jnp.max(t[hh][b], axis=1, keepdims=True) - base[hh][b] for b in blocks] for hh in two]
            shift = base
            if not diagonal:
                picked = [[jnp.sum(jnp.where(blane == j0 + b, sel[hh], 0.0), axis=1, keepdims=True) > 0.5
                           for b in blocks] for hh in two]
                cand = [[jnp.where(picked[hh][b], cand[hh][b], NEG) for b in blocks] for hh in two]
                shift = [[jnp.where(picked[hh][b], base[hh][b], -NEG) for b in blocks] for hh in two]
            m_new = [functools.reduce(jnp.maximum, cand[hh], m[hh]) for hh in two]
            ps = [[jnp.exp2(t[hh][b] - (m_new[hh] + shift[hh][b])) for b in blocks] for hh in two]
            p = [ps[hh][0] if nblk == 1 else jnp.concatenate(ps[hh], axis=1) for hh in two]
        alpha = [jnp.exp2(m[hh] - m_new[hh]) for hh in two]
        l = [alpha[hh] * carry[hh][1] + jnp.sum(p[hh], axis=1, keepdims=True) for hh in two]
        pv = [jnp.dot(p[hh].astype(BF16), v2, preferred_element_type=F32) for hh in two]
        return tuple((m_new[hh], l[hh], alpha[hh] * carry[hh][2] + pv[hh]) for hh in two)

    init = tuple((jnp.full((T, 1), NEG, F32), jnp.zeros((T, 1), F32), jnp.zeros((T, LANES), F32)) for _ in range(2))
    carry = chunk(i, 1, init, True)
    done = 0
    for w in WIDE:
        n = (i - done) // w
        carry = lax.fori_loop(0, n, lambda jw, cr, w=w, done=done: chunk(done + jw * w, w, cr, False), carry)
        done = done + n * w
    carry = lax.fori_loop(done, i, lambda j, cr: chunk(j, 1, cr, False), carry)
    (_, l_a, acc_a), (_, l_b, acc_b) = carry
    o_ref[...] = jnp.where(lo, acc_a / l_a, acc_b / l_b)


def _prompt_attn(mode, q, kb, vb, n_rows, *extra):
    nq = n_rows // ATT_TILE
    qspec = pl.BlockSpec((ATT_TILE, LANES), lambda hp, i, *_: (i, hp))
    kvspec = pl.BlockSpec((n_rows, LANES), lambda hp, i, *_: (0, hp))
    if mode == "fox":
        c, ct = extra
        in_specs = [qspec, kvspec, kvspec, pl.BlockSpec((ATT_TILE, N_HEADS), lambda hp, i: (i, 0)),
                    _const_spec((N_HEADS, n_rows))]
        args = (q, kb, vb, c, ct)
        n_prefetch = 0
    else:
        slopes, kmean = extra
        in_specs = [qspec, kvspec, kvspec, pl.BlockSpec((kmean.shape[0], LANES), lambda hp, i, *_: (0, hp))]
        args = (slopes, q, kb, vb, kmean)
        n_prefetch = 1
    return pl.pallas_call(
        functools.partial(_prompt_attn_kernel, mode=mode),
        grid_spec=pltpu.PrefetchScalarGridSpec(
            num_scalar_prefetch=n_prefetch, grid=(N_HEADS // 2, nq), in_specs=in_specs, out_specs=qspec),
        out_shape=jax.ShapeDtypeStruct((n_rows, WIDTH), F32),
        compiler_params=_cparams("arbitrary", "arbitrary"),
        name=mode + "_prompt",
    )(*args)


def _head_rows(x8):
    return jnp.concatenate([jnp.broadcast_to(x8[h:h + 1, :], (8, x8.shape[1])) for h in range(N_HEADS)], axis=0)


def _block_diag_q(q, scale):
    lane = lax.broadcasted_iota(jnp.int32, (1, WIDTH), 1)
    return jnp.concatenate([jnp.where(lane // HEAD_DIM == h, q * scale, 0.0) for h in range(N_HEADS)], axis=0)


def _pages_t(cache):
    d, n = cache.shape[:2]
    return jnp.transpose(cache, (0, 1, 3, 4, 2)).reshape(d, n, WIDTH, PAGE)


def _page_spec(layer, n_pages, i, reverse):
    def index(b, p, pt, *_):
        pg = p * PAGES_PER_STEP + i
        if reverse:
            pg = n_pages - 1 - pg
        return (layer, pt[b, pg], 0, 0)
    return pl.BlockSpec((None, None, WIDTH, PAGE), index)


def _sample_select_kernel(pt_ref, q_ref, knew_ref, *refs, n_pages):
    P = PAGES_PER_STEP
    k_refs, sel_ref, kmt_ref = refs[:P], refs[P], refs[P + 1]
    p = pl.program_id(1)
    per_blk = MOBA_BLOCK // PAGE
    lane = lax.broadcasted_iota(jnp.int32, (1, LANES), 1)

    @pl.when(p == 0)
    def _():
        kmt_ref[...] = jnp.zeros(kmt_ref.shape, F32)

    kmt = kmt_ref[...]
    for i in range(P // per_blk):
        tot = sum(k_refs[per_blk * i + t][...] for t in range(per_blk))
        col = jnp.sum(tot, axis=1, keepdims=True) * (1.0 / MOBA_BLOCK)
        kmt = jnp.where(lane == p * (P // per_blk) + i, col, kmt)
    kmt_ref[...] = kmt

    @pl.when(p == pl.num_programs(1) - 1)
    def _():
        n_past = n_pages // per_blk
        qbd = _block_diag_q(q_ref[...], 1.0)
        bs = _dot_hi(qbd, kmt_ref[...])
        own_mean = jnp.sum(knew_ref[...], axis=0, keepdims=True) * (1.0 / MOBA_BLOCK)
        bs = jnp.where(lane == n_past, jnp.sum(qbd * own_mean, axis=1, keepdims=True), bs)
        sel_ref[...] = _top3(bs, lane < n_past, n_past)


def _sample_select(page_table, q, knew, cache_kt, layer, row0):
    n_seq, n_pages = page_table.shape
    t = knew.shape[0] // n_seq
    P = PAGES_PER_STEP
    return pl.pallas_call(
        functools.partial(_sample_select_kernel, n_pages=n_pages),
        grid_spec=pltpu.PrefetchScalarGridSpec(
            num_scalar_prefetch=1, grid=(n_seq, n_pages // P),
            in_specs=[pl.BlockSpec((t, WIDTH), lambda b, p, pt: (row0 // t + b, 0)),
                      pl.BlockSpec((t, WIDTH), lambda b, p, pt: (b, 0))]
            + [_page_spec(layer, n_pages, i, False) for i in range(P)],
            out_specs=pl.BlockSpec((None, N_HEADS * t, LANES), lambda b, p, pt: (b, 0, 0)),
            scratch_shapes=[pltpu.VMEM((WIDTH, LANES), F32)]),
        out_shape=jax.ShapeDtypeStruct((n_seq, N_HEADS * t, LANES), F32),
        compiler_params=_cparams("arbitrary", "arbitrary"),
        name="moba_sample_select",
    )(page_table, q, knew, *([cache_kt] * P))


def _sample_attn_kernel(*refs, mode, n_pages):
    P = PAGES_PER_STEP
    if mode == "fox":
        pt_ref, q_ref, knew_ref, vnew_ref, lnew_ref = refs[:5]
        rest = refs[5:]
        lf_refs, rest = rest[:P], rest[P:]
    else:
        pt_ref, slopes_ref, q_ref, knew_ref, vnew_ref, sel_ref = refs[:6]
        rest = refs[6:]
    k_refs, v_refs, rest = rest[:P], rest[P:2 * P], rest[2 * P:]
    o_ref, qbd_ref, m_ref, l_ref, acc_ref, carry_ref, cq_ref = rest
    p = pl.program_id(1)
    R = qbd_ref.shape[0]
    t_new = R // N_HEADS
    lane = lax.broadcasted_iota(jnp.int32, (1, LANES), 1)
    trow = lax.broadcasted_iota(jnp.int32, (R, 1), 0) % t_new
    if mode == "moba":
        srow = jnp.concatenate([jnp.full((t_new, 1), slopes_ref[h], F32) for h in range(N_HEADS)], axis=0)

    @pl.when(p == 0)
    def _():
        qbd_ref[...] = _block_diag_q(q_ref[...], ATTN_SCALE).astype(BF16)
        pad = jnp.zeros((PAGE - t_new, WIDTH), F32)
        s = _dot_nt(qbd_ref[...], jnp.concatenate([knew_ref[...], pad], axis=0))
        if mode == "fox":
            x = lnew_ref[...]
            d = 1
            while d < t_new:
                x = x + jnp.where(lane >= d, pltpu.roll(x, d, 1), 0.0)
                d *= 2
            crep = _head_rows(x)
            cq = jnp.sum(jnp.where(lane == trow, crep, 0.0), axis=1, keepdims=True)
            cq_ref[...] = cq
            carry_ref[...] = jnp.zeros(carry_ref.shape, F32)
            s = s + cq - crep
        else:
            s = s - srow * (trow - lane).astype(F32)
        s = jnp.where(lane <= trow, s, NEG)
        m = jnp.max(s, axis=1, keepdims=True)
        pr = jnp.exp(s - m)
        m_ref[...] = m
        l_ref[...] = jnp.sum(pr, axis=1, keepdims=True)
        acc_ref[...] = _dot(pr, jnp.concatenate([vnew_ref[...], pad], axis=0))

    qbd = qbd_ref[...]
    parts = []
    if mode == "fox":
        carry = carry_ref[...]
    for i in range(P):
        pg = n_pages - 1 - (p * P + i)
        s = _dot(qbd, k_refs[i][...])
        if mode == "fox":
            lf = lf_refs[i][...]
            x = lf
            d = 1
            while d < PAGE:
                x = x + jnp.where(lane < PAGE - d, pltpu.roll(x, PAGE - d, 1), 0.0)
                d *= 2
            bias = (x - lf) + carry
            carry = carry + x[:, 0:1]
            s = s + _head_rows(bias) + cq_ref[...]
        else:
            dist = (n_pages * PAGE + trow) - (pg * PAGE + lane)
            s = s - srow * dist.astype(F32)
            picked = jnp.sum(jnp.where(lane == pg // (MOBA_BLOCK // PAGE), sel_ref[...], 0.0), axis=1, keepdims=True) > 0.5
            s = jnp.where(picked, s, NEG)
        parts.append(s)
    if mode == "fox":
        carry_ref[...] = carry
    s_all = jnp.concatenate(parts, axis=1)
    m_old = m_ref[...]
    m_new = jnp.maximum(m_old, jnp.max(s_all, axis=1, keepdims=True))
    alpha = jnp.exp(m_old - m_new)
    pr = jnp.exp(s_all - m_new)
    m_ref[...] = m_new
    l_ref[...] = alpha * l_ref[...] + jnp.sum(pr, axis=1, keepdims=True)
    pv = sum(_dot_nt(pr[:, i * PAGE:(i + 1) * PAGE], v_refs[i][...]) for i in range(P))
    acc_ref[...] = alpha * acc_ref[...] + pv

    @pl.when(p == pl.num_programs(1) - 1)
    def _():
        res = acc_ref[...] / l_ref[...]
        wl = lax.broadcasted_iota(jnp.int32, (1, WIDTH), 1)
        out = jnp.zeros((t_new, WIDTH), F32)
        for h in range(N_HEADS):
            out = jnp.where(wl // HEAD_DIM == h, res[h * t_new:(h + 1) * t_new, :], out)
        o_ref[...] = out


def _sample_attn(mode, page_table, q, knew, vnew, cache_kt, cache_vt, layer, row0, *extra):
    n_seq, n_pages = page_table.shape
    t = knew.shape[0] // n_seq
    R = N_HEADS * t
    P = PAGES_PER_STEP
    nsp = 1 if mode == "fox" else 2
    qtok = pl.BlockSpec((t, WIDTH), lambda b, p, *_: (row0 // t + b, 0))
    tok = pl.BlockSpec((t, WIDTH), lambda b, p, *_: (b, 0))
    pages = [_page_spec(layer, n_pages, i, True) for i in range(P)]
    if mode == "fox":
        lnew_t, lf_t = extra
        def lf_spec(i):
            return pl.BlockSpec((None, N_HEADS, PAGE), lambda b, p, pt: (pt[b, n_pages - 1 - (p * P + i)], 0, 0))
        in_specs = ([qtok, tok, tok, pl.BlockSpec((None, N_HEADS, LANES), lambda b, p, pt: (b, 0, 0))]
                    + [lf_spec(i) for i in range(P)] + pages + pages)
        args = (page_table, q, knew, vnew, lnew_t, *([lf_t] * P), *([cache_kt] * P), *([cache_vt] * P))
    else:
        slopes, sel = extra
        in_specs = [qtok, tok, tok, pl.BlockSpec((None, R, LANES), lambda b, p, pt, sl: (b, 0, 0))] + pages + pages
        args = (page_table, slopes, q, knew, vnew, sel, *([cache_kt] * P), *([cache_vt] * P))
    return pl.pallas_call(
        functools.partial(_sample_attn_kernel, mode=mode, n_pages=n_pages),
        grid_spec=pltpu.PrefetchScalarGridSpec(
            num_scalar_prefetch=nsp, grid=(n_seq, n_pages // P), in_specs=in_specs, out_specs=tok,
            scratch_shapes=[pltpu.VMEM((R, WIDTH), BF16), pltpu.VMEM((R, 1), F32), pltpu.VMEM((R, 1), F32),
                            pltpu.VMEM((R, WIDTH), F32), pltpu.VMEM((N_HEADS, 1), F32), pltpu.VMEM((R, 1), F32)]),
        out_shape=jax.ShapeDtypeStruct((n_seq * t, WIDTH), F32),
        compiler_params=_cparams("arbitrary", "arbitrary"),
        name=mode + "_sample",
    )(*args)


def _merge_kernel(h_ref, gate_ref, orw_ref, ofox_ref, omoba_ref, wr_ref, wf_ref, wm_ref, wo_ref, g_ref, o_ref):
    m = (gate_ref[:, 0:D_MODEL] * _dot(orw_ref[...], wr_ref[...])
         + gate_ref[:, D_MODEL:2 * D_MODEL] * _dot(ofox_ref[...], wf_ref[...])
         + gate_ref[:, 2 * D_MODEL:3 * D_MODEL] * _dot(omoba_ref[...], wm_ref[...]))
    o_ref[...] = h_ref[...] + _rms(_dot(m, wo_ref[...]), g_ref[...])


def _merge(h, gates, o_rw, o_fox, o_moba, wr, wf, wm, wo, g):
    m = h.shape[0]

    def row(n):
        return pl.BlockSpec((ROW_TILE, n), lambda i: (i, 0))

    wspec = _const_spec((WIDTH, D_MODEL))
    return pl.pallas_call(
        _merge_kernel,
        grid=(m // ROW_TILE,),
        in_specs=[row(D_MODEL), row(3 * D_MODEL), row(WIDTH), row(WIDTH), row(WIDTH), wspec, wspec, wspec,
                  _const_spec((D_MODEL, D_MODEL)), _const_spec((1, D_MODEL))],
        out_specs=row(D_MODEL),
        out_shape=jax.ShapeDtypeStruct((m, D_MODEL), F32),
        compiler_params=_cparams("arbitrary"),
        name="merge",
    )(h, gates, o_rw, o_fox, o_moba, wr, wf, wm, wo, g)


def _pack_w_in(w_in, b_forget):
    o = 0
    def take(n):
        nonlocal o
        s = w_in[:, o:o + n]
        o += n
        return s
    rkv = take(3 * WIDTH)
    wd, ad, gd = take(W_LORA), take(A_LORA), take(G_LORA)
    fox = take(3 * WIDTH)
    fl = take(N_HEADS)
    moba = take(3 * WIDTH)
    gates = take(3 * D_MODEL)
    z = lambda n: jnp.zeros((D_MODEL, n), w_in.dtype)
    packed = jnp.concatenate([rkv, wd, z(LANES - W_LORA), ad, z(LANES - A_LORA), gd, fox, moba, gates,
                              fl, z(LANES - N_HEADS)], axis=1)
    bf = jnp.concatenate([b_forget, jnp.zeros((LANES - N_HEADS,), F32)])[None, :]
    return packed.astype(BF16), bf


def _pad_rows(w, n):
    return jnp.concatenate([w, jnp.zeros((n - w.shape[0], w.shape[1]), w.dtype)], axis=0)


def _pack_mu(mu):
    z = jnp.zeros((LANES - W_LORA,), F32)
    return jnp.concatenate([mu[:3 * WIDTH], mu[3 * WIDTH:3 * WIDTH + W_LORA], z,
                            mu[3 * WIDTH + W_LORA:3 * WIDTH + W_LORA + A_LORA], z,
                            mu[3 * WIDTH + W_LORA + A_LORA:]])[None, :]


def _block_diag_state(s):
    b = s.shape[0]
    eye = jnp.eye(N_HEADS, dtype=s.dtype)
    return jnp.einsum('bhij,hg->bhigj', s, eye).reshape(b, WIDTH, WIDTH)


def _unblock_state(s):
    b = s.shape[0]
    s5 = s.reshape(b, N_HEADS, HEAD_DIM, N_HEADS, HEAD_DIM)
    return jnp.stack([s5[:, h, :, h, :] for h in range(N_HEADS)], axis=1)


def kernel(x_prompt, x_sample, cache_fox_k, cache_fox_v, cache_fox_logf, cache_moba_k, cache_moba_v, state_rwkv_wkv, state_rwkv_shift, page_table, norms, w_in, b_forget, rwkv_mu, rwkv_w0, rwkv_w_up, rwkv_a0, rwkv_a_up, rwkv_g_up, rwkv_k_k, rwkv_k_a, rwkv_r_k, rwkv_ln_w, rwkv_ln_b, w_proj_rwkv, w_proj_fox, w_proj_moba, w_out, ffn_gate, ffn_up, ffn_down):
    depth = norms.shape[0]
    n_p = x_prompt.shape[0] * x_prompt.shape[1]
    n_seq, t_s = x_sample.shape[0], x_sample.shape[1]
    n_s = n_seq * t_s
    assert x_prompt.shape[0] == 1 and n_p % ROW_TILE == 0 and n_s % ROW_TILE == 0 and n_p % (8 * MOBA_BLOCK) == 0

    x = jnp.concatenate([x_prompt.reshape(n_p, D_MODEL), x_sample.reshape(n_s, D_MODEL)], axis=0)
    slopes = 2.0 ** (-8.0 * jnp.arange(1, N_HEADS + 1, dtype=F32) / N_HEADS)
    hid = jnp.arange(WIDTH) // HEAD_DIM
    bd = (hid[:, None] == hid[None, :]).astype(BF16)
    ar = jnp.arange(LANES)
    tri = (ar[:, None] <= ar[None, :]).astype(BF16)
    n_rows_c = N_HEADS * n_p // LANES
    rr = jnp.arange(n_rows_c)
    blk = ((rr[:, None] // (n_p // LANES) == rr[None, :] // (n_p // LANES)) & (rr[None, :] < rr[:, None])).astype(BF16)
    cfk, cfv, cmk, cmv = (_pages_t(c) for c in (cache_fox_k, cache_fox_v, cache_moba_k, cache_moba_v))
    vec = lambda a: a.reshape(1, -1)

    outs_p, outs_s = [], []
    for l in range(depth):
        bfc = lambda a: a.astype(BF16)
        h = _ffn(x, vec(norms[l, 0]), vec(norms[l, 1]), bfc(ffn_gate[l, 0]), bfc(ffn_up[l, 0]), bfc(ffn_down[l, 0]))
        w_pack, bf_pad = _pack_w_in(w_in[l], b_forget[l])
        (u, rw, fq, fk, fv, fkb, fvb, logf, mq, mk, mv, mkb, mvb, gates) = _inproj(h, vec(norms[l, 2]), w_pack, bf_pad)

        prev_u = jnp.concatenate([jnp.zeros((8, D_MODEL), F32), state_rwkv_shift[l]], axis=0)
        prev_rw = _mm(prev_u, w_pack[:, :RW_PACK])
        rparams = (_pack_mu(rwkv_mu[l]), vec(rwkv_w0[l]), bfc(_pad_rows(rwkv_w_up[l], LANES)), vec(rwkv_a0[l]),
                   bfc(_pad_rows(rwkv_a_up[l], LANES)), bfc(rwkv_g_up[l]), vec(rwkv_k_k[l]), vec(rwkv_k_a[l]),
                   vec(rwkv_r_k[l]), vec(rwkv_ln_w[l]), vec(rwkv_ln_b[l]))
        o_rw_p, s_p = _rwkv(rw, prev_rw[0:1][None], jnp.zeros((1, WIDTH, WIDTH), F32), rparams, bd,
                            n_seq=1, t_seq=n_p, row0=0)
        o_rw_s, s_s = _rwkv(rw, prev_rw[8:8 + n_seq][:, None, :], _block_diag_state(state_rwkv_wkv[l]), rparams, bd,
                            n_seq=n_seq, t_seq=t_s, row0=n_p)

        lf_t = logf[:n_p, :N_HEADS].T
        ct = _cumsum_rows(lf_t.reshape(n_rows_c, LANES), tri, blk).reshape(N_HEADS, n_p)
        o_fox_p = _prompt_attn("fox", fq, fkb, fvb, n_p, ct.T, ct)
        lnew_t = jnp.swapaxes(logf[n_p:, :N_HEADS].reshape(n_seq, t_s, N_HEADS), 1, 2)
        lnew_t = jnp.concatenate([lnew_t, jnp.zeros((n_seq, N_HEADS, LANES - t_s), F32)], axis=2)
        lf_pool_t = jnp.swapaxes(cache_fox_logf[l].astype(F32), 1, 2)
        o_fox_s = _sample_attn("fox", page_table, fq, fk[n_p:], fv[n_p:], cfk, cfv, l, n_p, lnew_t, lf_pool_t)

        kmean = _kmean(mk, n_p)
        o_moba_p = _prompt_attn("moba", mq, mkb, mvb, n_p, slopes, kmean)
        sel = _sample_select(page_table, mq, mk[n_p:], cmk, l, n_p)
        o_moba_s = _sample_attn("moba", page_table, mq, mk[n_p:], mv[n_p:], cmk, cmv, l, n_p, slopes, sel)

        cat = lambda a, b: jnp.concatenate([a, b], axis=0)
        h = _merge(h, gates, cat(o_rw_p, o_rw_s), cat(o_fox_p, o_fox_s), cat(o_moba_p, o_moba_s),
                   bfc(w_proj_rwkv[l]), bfc(w_proj_fox[l]), bfc(w_proj_moba[l]), bfc(w_out[l]), vec(norms[l, 3]))
        x = _ffn(h, vec(norms[l, 4]), vec(norms[l, 5]), bfc(ffn_gate[l, 1]), bfc(ffn_up[l, 1]), bfc(ffn_down[l, 1]))

        hd = lambda a, b, t: a.reshape(b, t, N_HEADS, HEAD_DIM)
        outs_p.append((hd(fk[:n_p], 1, n_p), hd(fv[:n_p], 1, n_p), logf[:n_p, :N_HEADS].reshape(1, n_p, N_HEADS),
                       hd(mk[:n_p], 1, n_p), hd(mv[:n_p], 1, n_p), _unblock_state(s_p), u[n_p - 1:n_p]))
        outs_s.append((hd(fk[n_p:], n_seq, t_s), hd(fv[n_p:], n_seq, t_s),
                       logf[n_p:, :N_HEADS].reshape(n_seq, t_s, N_HEADS), hd(mk[n_p:], n_seq, t_s),
                       hd(mv[n_p:], n_seq, t_s), _unblock_state(s_s),
                       u[n_p:].reshape(n_seq, t_s, D_MODEL)[:, -1]))

    st = lambda rows, i: jnp.stack([r[i] for r in rows])
    return (x[:n_p].reshape(x_prompt.shape), x[n_p:].reshape(x_sample.shape),
            *[st(outs_p, i) for i in range(7)], *[st(outs_s, i) for i in range(7)])
```

```python
import functools

import jax
import jax.numpy as jnp
from jax import lax
from jax.experimental import pallas as pl
from jax.experimental.pallas import tpu as pltpu

F32 = jnp.float32
BF16 = jnp.bfloat16

D_MODEL = 1024
HEAD_DIM = 64
N_HEADS = 8
WIDTH = N_HEADS * HEAD_DIM
W_LORA, A_LORA, G_LORA = 64, 64, 128
D_FF = 2816
PAGE = 128
MOBA_BLOCK = 256
MOBA_TOPK = 3
RMS_EPS = 1e-6
GN_EPS = 64e-5
ATTN_SCALE = HEAD_DIM ** -0.5
NEG = -1e30
LANES = 128
RW_PACK = 3 * WIDTH + 3 * LANES
IN_PACK = RW_PACK + 6 * WIDTH + 3 * D_MODEL + LANES
ROW_TILE = 256
RW_CHUNK = 128
RW_CHUNK_MIN = 32
ATT_TILE = 256
WIDE = (8, 4)
PAGES_PER_STEP = 16
LOG2E = 1.4426950408889634
VMEM_LIMIT = 56 * 1024 * 1024


def _cparams(*sem):
    return pltpu.CompilerParams(dimension_semantics=sem, vmem_limit_bytes=VMEM_LIMIT)


def _dot(a, b):
    return jnp.dot(a.astype(BF16), b.astype(BF16), preferred_element_type=F32)


def _dot_nt(a, b):
    return lax.dot_general(a.astype(BF16), b.astype(BF16), (((1,), (1,)), ((), ())), preferred_element_type=F32)


def _dot_tn(a, b):
    return lax.dot_general(a.astype(BF16), b.astype(BF16), (((0,), (0,)), ((), ())), preferred_element_type=F32)


def _split3(a):
    a1 = a.astype(BF16)
    r1 = a - a1.astype(F32)
    a2 = r1.astype(BF16)
    a3 = (r1 - a2.astype(F32)).astype(BF16)
    return a1, a2, a3


def _dot_exact_rhs(a, b01):
    a1, a2, a3 = _split3(a)
    return _dot(a1, b01) + _dot(a2, b01) + _dot(a3, b01)


def _dot_exact_lhs(a01, b):
    b1, b2, b3 = _split3(b)
    return _dot(a01, b1) + _dot(a01, b2) + _dot(a01, b3)


def _dot_nt_hi(a, b):
    a1, a2, a3 = _split3(a)
    b1, b2, b3 = _split3(b)
    return (_dot_nt(a1, b1) + _dot_nt(a1, b2) + _dot_nt(a2, b1)
            + _dot_nt(a1, b3) + _dot_nt(a2, b2) + _dot_nt(a3, b1))


def _dot_hi(a, b):
    a1, a2, a3 = _split3(a)
    b1, b2, b3 = _split3(b)
    return _dot(a1, b1) + _dot(a1, b2) + _dot(a2, b1) + _dot(a1, b3) + _dot(a2, b2) + _dot(a3, b1)


def _rms(x, g):
    return x * lax.rsqrt(jnp.mean(x * x, axis=-1, keepdims=True) + RMS_EPS) * g


def _sigmoid(x):
    return 1.0 / (1.0 + jnp.exp(-x))


def _softplus(x):
    return jnp.maximum(x, 0.0) + jnp.log1p(jnp.exp(-jnp.abs(x)))


def _const_spec(shape):
    nd = len(shape)
    return pl.BlockSpec(shape, lambda *_: (0,) * nd, pipeline_mode=pl.Buffered(1))


def _ffn_kernel(x_ref, gpre_ref, gpost_ref, wg_ref, wu_ref, wd_ref, o_ref):
    x = x_ref[...]
    xn = _rms(x, gpre_ref[...]).astype(BF16)
    g = jnp.dot(xn, wg_ref[...], preferred_element_type=F32)
    u = jnp.dot(xn, wu_ref[...], preferred_element_type=F32)
    a = (g * _sigmoid(g) * u).astype(BF16)
    y = jnp.dot(a, wd_ref[...], preferred_element_type=F32)
    o_ref[...] = x + 0.5 * _rms(y, gpost_ref[...])


def _ffn(x, gpre, gpost, wg, wu, wd):
    m = x.shape[0]
    row = pl.BlockSpec((ROW_TILE, D_MODEL), lambda i: (i, 0))
    return pl.pallas_call(
        _ffn_kernel,
        grid=(m // ROW_TILE,),
        in_specs=[row, _const_spec((1, D_MODEL)), _const_spec((1, D_MODEL)),
                  _const_spec((D_MODEL, D_FF)), _const_spec((D_MODEL, D_FF)), _const_spec((D_FF, D_MODEL))],
        out_specs=row,
        out_shape=jax.ShapeDtypeStruct((m, D_MODEL), F32),
        compiler_params=_cparams("arbitrary"),
        name="ffn",
    )(x, gpre, gpost, wg, wu, wd)


def _inproj_kernel(h_ref, g_ref, w_ref, bf_ref, u_ref, rw_ref, fq_ref, fk_ref, fv_ref, fkb_ref, fvb_ref,
                   lf_ref, mq_ref, mk_ref, mv_ref, mkb_ref, mvb_ref, gate_ref):
    u = _rms(h_ref[...], g_ref[...])
    u_ref[...] = u
    ub = u.astype(BF16)

    def proj(lo, n):
        return jnp.dot(ub, w_ref[:, lo:lo + n], preferred_element_type=F32)

    rw_ref[...] = proj(0, RW_PACK)
    o = RW_PACK
    fq_ref[...] = proj(o, WIDTH)
    fk = proj(o + WIDTH, WIDTH)
    fv = proj(o + 2 * WIDTH, WIDTH)
    fk_ref[...] = fk
    fv_ref[...] = fv
    fkb_ref[...] = fk.astype(BF16)
    fvb_ref[...] = fv.astype(BF16)
    o += 3 * WIDTH
    mq_ref[...] = proj(o, WIDTH)
    mk = proj(o + WIDTH, WIDTH)
    mv = proj(o + 2 * WIDTH, WIDTH)
    mk_ref[...] = mk
    mv_ref[...] = mv
    mkb_ref[...] = mk.astype(BF16)
    mvb_ref[...] = mv.astype(BF16)
    o += 3 * WIDTH
    gate_ref[...] = _sigmoid(proj(o, 3 * D_MODEL))
    o += 3 * D_MODEL
    lf_ref[...] = -_softplus(-(proj(o, LANES) + bf_ref[...]))


def _inproj(h, g, w_pack, bf_pad):
    m = h.shape[0]

    def row(n):
        return pl.BlockSpec((ROW_TILE, n), lambda i: (i, 0))

    widths = [(D_MODEL, F32), (RW_PACK, F32), (WIDTH, F32), (WIDTH, F32), (WIDTH, F32), (WIDTH, BF16), (WIDTH, BF16),
              (LANES, F32), (WIDTH, F32), (WIDTH, F32), (WIDTH, F32), (WIDTH, BF16), (WIDTH, BF16), (3 * D_MODEL, F32)]
    return pl.pallas_call(
        _inproj_kernel,
        grid=(m // ROW_TILE,),
        in_specs=[row(D_MODEL), _const_spec((1, D_MODEL)), _const_spec((D_MODEL, IN_PACK)), _const_spec((1, LANES))],
        out_specs=[row(n) for n, _ in widths],
        out_shape=[jax.ShapeDtypeStruct((m, n), dt) for n, dt in widths],
        compiler_params=_cparams("arbitrary"),
        name="inproj",
    )(h, g, w_pack, bf_pad)


def _mm_kernel(a_ref, b_ref, o_ref):
    o_ref[...] = jnp.dot(a_ref[...].astype(BF16), b_ref[...], preferred_element_type=F32)


def _mm(a, b):
    return pl.pallas_call(
        _mm_kernel,
        out_shape=jax.ShapeDtypeStruct((a.shape[0], b.shape[1]), F32),
        compiler_params=pltpu.CompilerParams(vmem_limit_bytes=VMEM_LIMIT),
        name="shift_proj",
    )(a, b)


def _rwkv_kernel(rw_ref, prev_ref, s0_ref, mu_ref, w0_ref, wup_ref, a0_ref, aup_ref, gup_ref, kk_ref, ka_ref,
                 rk_ref, lnw_ref, lnb_ref, bd_ref, o_ref, sout_ref, last_ref, s_ref, *, t_real, C):
    c = pl.program_id(1)

    @pl.when(c == 0)
    def _():
        last_ref[...] = prev_ref[...]
        s_ref[...] = s0_ref[...]

    rw = rw_ref[...]
    if t_real < C:
        rw = jnp.concatenate([rw, jnp.zeros((C - t_real, RW_PACK), F32)], axis=0)
    row = lax.broadcasted_iota(jnp.int32, (C, 1), 0)
    rw_prev = jnp.where(row == 0, last_ref[...], pltpu.roll(rw, 1, 0))
    last_ref[...] = rw[C - 1:C, :]
    xm = rw + (rw_prev - rw) * mu_ref[...]
    if t_real < C:
        xm = jnp.where(row < t_real, xm, 0.0)
    r = xm[:, 0:WIDTH]
    k = xm[:, WIDTH:2 * WIDTH]
    v = xm[:, 2 * WIDTH:3 * WIDTH]
    wd = xm[:, 3 * WIDTH:3 * WIDTH + LANES]
    ad = xm[:, 3 * WIDTH + LANES:3 * WIDTH + 2 * LANES]
    gd = xm[:, 3 * WIDTH + 2 * LANES:RW_PACK]
    bd = bd_ref[...]

    z = w0_ref[...] + _dot(jnp.tanh(wd), wup_ref[...])
    lw = -jnp.exp(-_softplus(-z) - 0.5)
    if t_real < C:
        lw = jnp.where(row < t_real, lw, 0.0)
    ag = _sigmoid(a0_ref[...] + _dot(ad, aup_ref[...]))
    g = _dot(_sigmoid(gd), gup_ref[...])
    kk = k * kk_ref[...]
    kk = kk / jnp.maximum(jnp.sqrt(_dot_exact_rhs(kk * kk, bd)), 1e-12)
    kh = k * (1.0 + (ag - 1.0) * ka_ref[...])
    a_ = -kk
    b_ = kk * ag

    ti = lax.broadcasted_iota(jnp.int32, (C, C), 0)
    si = lax.broadcasted_iota(jnp.int32, (C, C), 1)
    incl = si <= ti
    strict = si < ti
    cum = _dot_exact_lhs(incl.astype(BF16), lw)
    tot = cum[C - 1:C, :]
    p_end = jnp.exp(tot - cum)
    p_inv = jnp.exp(-cum)
    at = a_ * jnp.exp(cum - lw)
    rt = r * jnp.exp(cum)
    bt = (b_ * p_inv).astype(BF16)
    kt = (kh * p_inv).astype(BF16)
    bh = b_ * p_end
    khat = kh * p_end
    vb = v.astype(BF16)
    eye = (si == ti).astype(F32)
    lane = lax.broadcasted_iota(jnp.int32, (1, LANES), 1)

    heads = range(N_HEADS)
    pair = lambda x, h: x[:, (h // 2) * LANES:(h // 2 + 1) * LANES]
    mh = [(lane < HEAD_DIM) if h % 2 == 0 else (lane >= HEAD_DIM) for h in heads]
    at_h = [jnp.where(mh[h], pair(at, h), 0.0).astype(BF16) for h in heads]
    rt_h = [jnp.where(mh[h], pair(rt, h), 0.0).astype(BF16) for h in heads]
    a_ab = [jnp.where(strict, _dot_nt(at_h[h], pair(bt, h)), 0.0) for h in heads]
    a_ak = [jnp.where(strict, _dot_nt(at_h[h], pair(kt, h)), 0.0) for h in heads]
    a_rb = [jnp.where(incl, _dot_nt(rt_h[h], pair(bt, h)), 0.0) for h in heads]
    a_rk = [jnp.where(incl, _dot_nt(rt_h[h], pair(kt, h)), 0.0) for h in heads]
    tm = [eye + a_ab[h] for h in heads]
    apow = a_ab
    n = 2
    while n < C:
        apow = [_dot(apow[h], apow[h]) for h in heads]
        tm = [tm[h] + _dot(tm[h], apow[h]) for h in heads]
        n *= 2
    ua = [_dot(tm[h], pair(at, h)) for h in heads]
    akv = [_dot(a_ak[h], pair(vb, h)) for h in heads]
    uv = [_dot(tm[h], akv[h]) for h in heads]
    qy = [pair(rt, h) + _dot(a_rb[h], ua[h]) for h in heads]
    yc = [_dot(a_rb[h], uv[h]) + _dot(a_rk[h], pair(vb, h)) for h in heads]
    merge = lambda xs: jnp.concatenate([jnp.where(mh[0], xs[h], xs[h + 1]) for h in range(0, N_HEADS, 2)], axis=1)
    ua_all, uv_all, qy_all, yc_all = merge(ua), merge(uv), merge(qy), merge(yc)

    s0 = s_ref[...]
    y = _dot_nt(qy_all, s0) + yc_all
    bdm = bd > 0
    gc = jnp.where(bdm, _dot_tn(ua_all, bh), 0.0)
    hm = jnp.where(bdm, _dot_tn(uv_all, bh) + _dot_tn(v, khat), 0.0)
    s_new = s0 * jnp.exp(tot) + _dot(s0, gc) + hm
    s_ref[...] = s_new
    sout_ref[...] = s_new

    inv_n = 1.0 / HEAD_DIM
    mean = _dot_exact_rhs(y, bd) * inv_n
    yd = y - mean
    var = _dot_exact_rhs(yd * yd, bd) * inv_n
    yn = yd * lax.rsqrt(var + GN_EPS) * lnw_ref[...] + lnb_ref[...]
    bonus = _dot_exact_rhs(r * kh * rk_ref[...], bd) * v
    o = (yn + bonus) * g
    o_ref[...] = o[:t_real] if t_real < C else o


def _rwkv(rw, prev, s0bd, params, bd, *, n_seq, t_seq, row0):
    t_real = min(t_seq, RW_CHUNK)
    n_chunk = max(t_seq // RW_CHUNK, 1)
    chunk = RW_CHUNK if t_seq >= RW_CHUNK else max(RW_CHUNK_MIN, t_seq)
    blk0 = row0 // t_real

    def vec(n):
        return _const_spec((1, n))

    in_specs = [pl.BlockSpec((t_real, RW_PACK), lambda b, c: (blk0 + b * n_chunk + c, 0)),
                pl.BlockSpec((None, 1, RW_PACK), lambda b, c: (b, 0, 0)),
                pl.BlockSpec((None, WIDTH, WIDTH), lambda b, c: (b, 0, 0)),
                vec(RW_PACK), vec(WIDTH), _const_spec((LANES, WIDTH)), vec(WIDTH), _const_spec((LANES, WIDTH)),
                _const_spec((G_LORA, WIDTH)), vec(WIDTH), vec(WIDTH), vec(WIDTH), vec(WIDTH), vec(WIDTH),
                _const_spec((WIDTH, WIDTH))]
    return pl.pallas_call(
        functools.partial(_rwkv_kernel, t_real=t_real, C=chunk),
        grid=(n_seq, n_chunk),
        in_specs=in_specs,
        out_specs=[pl.BlockSpec((t_real, WIDTH), lambda b, c: (b * n_chunk + c, 0)),
                   pl.BlockSpec((None, WIDTH, WIDTH), lambda b, c: (b, 0, 0))],
        out_shape=[jax.ShapeDtypeStruct((n_seq * t_seq, WIDTH), F32),
                   jax.ShapeDtypeStruct((n_seq, WIDTH, WIDTH), F32)],
        scratch_shapes=[pltpu.VMEM((1, RW_PACK), F32), pltpu.VMEM((WIDTH, WIDTH), F32)],
        compiler_params=_cparams("arbitrary", "arbitrary"),
        name="rwkv",
    )(rw, prev, s0bd, *params, bd)


def _cumsum_kernel(x_ref, tri_ref, blk_ref, o_ref):
    w = _dot_exact_rhs(x_ref[...], tri_ref[...])
    tot = jnp.broadcast_to(w[:, LANES - 1:LANES], w.shape)
    o_ref[...] = (w + _dot_exact_lhs(blk_ref[...], tot)) * LOG2E


def _cumsum_rows(x, tri, blk):
    return pl.pallas_call(
        _cumsum_kernel,
        out_shape=jax.ShapeDtypeStruct(x.shape, F32),
        compiler_params=pltpu.CompilerParams(vmem_limit_bytes=VMEM_LIMIT),
        name="fox_cumsum",
    )(x, tri, blk)


def _kmean_kernel(k_ref, o_ref):
    n = k_ref.shape[0] // MOBA_BLOCK
    o_ref[...] = jnp.sum(k_ref[...].reshape(n, MOBA_BLOCK, WIDTH), axis=1) * (1.0 / MOBA_BLOCK)


def _kmean(k, n_rows):
    per = 8
    nb = n_rows // MOBA_BLOCK
    return pl.pallas_call(
        _kmean_kernel,
        grid=(nb // per,),
        in_specs=[pl.BlockSpec((per * MOBA_BLOCK, WIDTH), lambda i: (i, 0))],
        out_specs=pl.BlockSpec((per, WIDTH), lambda i: (i, 0)),
        out_shape=jax.ShapeDtypeStruct((nb, WIDTH), F32),
        compiler_params=_cparams("arbitrary"),
        name="moba_kmean",
    )(k)


def _top3(bs, valid, own):
    lane = lax.broadcasted_iota(jnp.int32, bs.shape, 1).astype(F32)
    rem = jnp.where(valid, bs, NEG)
    sel = jnp.zeros(bs.shape, F32)
    for r in range(MOBA_TOPK):
        mx = jnp.max(rem, axis=1, keepdims=True)
        idx = jnp.min(jnp.where(rem == mx, lane, float(bs.shape[1])), axis=1, keepdims=True)
        pick = lane == idx
        sel = jnp.where(pick, jnp.where(own > r, 1.0, 0.0), sel)
        rem = jnp.where(pick, -jnp.inf, rem)
    return sel


def _lockstep(gens):
    results = [None] * len(gens)
    live = list(range(len(gens)))
    while live:
        for k in list(live):
            try:
                next(gens[k])
            except StopIteration as e:
                results[k] = e.value
                live.remove(k)
    return results


def _both_kernel(slopes_ref, qf, kf, vf, c, ct, qm, km, vm, kmean, of, om):
    i = pl.program_id(1)
    ch_f, init, lo = _prompt_attn_kernel(qf, kf, vf, c, ct, of, mode="fox", mode_parts=True)
    ch_m, _, _ = _prompt_attn_kernel(slopes_ref, qm, km, vm, kmean, om, mode="moba", mode_parts=True)
    both = lambda j0, n, cr, d: tuple(_lockstep([ch_f(j0, n, cr[0], d), ch_m(j0, n, cr[1], d)]))
    carry = both(i, 1, (init, init), True)
    done = 0
    for w in WIDE:
        n = (i - done) // w
        carry = lax.fori_loop(0, n, lambda jw, cr, w=w, done=done: both(done + jw * w, w, cr, False), carry)
        done = done + n * w
    carry = lax.fori_loop(done, i, lambda j, cr: both(j, 1, cr, False), carry)
    for o_ref, ((_, l_a, acc_a), (_, l_b, acc_b)) in ((of, carry[0]), (om, carry[1])):
        o_ref[...] = jnp.where(lo, acc_a / l_a, acc_b / l_b)


def _both_prompt(fq, fkb, fvb, c, ct, mq, mkb, mvb, slopes, kmean, n_rows):
    nq = n_rows // ATT_TILE
    qspec = pl.BlockSpec((ATT_TILE, LANES), lambda hp, i, *_: (i, hp))
    kvspec = pl.BlockSpec((n_rows, LANES), lambda hp, i, *_: (0, hp))
    in_specs = [qspec, kvspec, kvspec, pl.BlockSpec((ATT_TILE, N_HEADS), lambda hp, i, *_: (i, 0)),
                pl.BlockSpec((N_HEADS, n_rows), lambda hp, i, *_: (0, 0), pipeline_mode=pl.Buffered(1)),
                qspec, kvspec, kvspec, pl.BlockSpec((kmean.shape[0], LANES), lambda hp, i, *_: (0, hp))]
    return pl.pallas_call(
        _both_kernel,
        grid_spec=pltpu.PrefetchScalarGridSpec(
            num_scalar_prefetch=1, grid=(N_HEADS // 2, nq), in_specs=in_specs, out_specs=[qspec, qspec]),
        out_shape=[jax.ShapeDtypeStruct((n_rows, WIDTH), F32)] * 2,
        compiler_params=_cparams("arbitrary", "arbitrary"),
        name="both_prompt",
    )(slopes, fq, fkb, fvb, c, ct, mq, mkb, mvb, kmean)


def _prompt_attn_kernel(*refs, mode, mode_parts=False):
    if mode == "fox":
        q_ref, k_ref, v_ref, c_ref, ct_ref, o_ref = refs
    else:
        slopes_ref, q_ref, k_ref, v_ref, kmean_ref, o_ref = refs
    T = ATT_TILE
    hp = pl.program_id(0)
    i = pl.program_id(1)
    lane = lax.broadcasted_iota(jnp.int32, (1, LANES), 1)
    lo = lane < HEAD_DIM
    q = q_ref[...]
    qs = q * (ATTN_SCALE * LOG2E)
    qh = (jnp.where(lo, qs, 0.0).astype(BF16), jnp.where(lo, 0.0, qs).astype(BF16))
    ri = lax.broadcasted_iota(jnp.int32, (T, T), 0)
    ci = lax.broadcasted_iota(jnp.int32, (T, T), 1)
    causal = ci <= ri

    if mode == "fox":
        lane8 = lax.broadcasted_iota(jnp.int32, (1, N_HEADS), 1)
        cblk = c_ref[...]
        cq = [jnp.sum(jnp.where(lane8 == 2 * hp + hh, cblk, 0.0), axis=1, keepdims=True) for hh in range(2)]
    else:
        d0 = (ri - ci).astype(F32)
        slope = [slopes_ref[2 * hp + hh] * LOG2E for hh in range(2)]
        sd0 = [slope[hh] * d0 for hh in range(2)]
        nb = kmean_ref.shape[0]
        kmean = kmean_ref[...]
        blane = lax.broadcasted_iota(jnp.int32, (1, nb), 1)
        sel = []
        for hh in range(2):
            q_h = jnp.where(lo, q, 0.0) if hh == 0 else jnp.where(lo, 0.0, q)
            sel.append(_top3(_dot_nt_hi(q_h, kmean), blane < i, i))

    def chunk_gen(j0, nblk, carry, diagonal):
        off = pl.multiple_of(j0 * T, T)
        k2 = k_ref[pl.ds(off, nblk * T), :]
        v2 = v_ref[pl.ds(off, nblk * T), :]
        two = range(2)
        m = [carry[hh][0] for hh in two]
        s = [_dot_nt(qh[hh], k2) for hh in two]
        yield
        if mode == "fox":
            t = [s[hh] - ct_ref[pl.ds(2 * hp + hh, 1), pl.ds(off, nblk * T)] for hh in two]
            if diagonal:
                t = [jnp.where(causal, t[hh], NEG) for hh in two]
            m_new = [jnp.maximum(m[hh], jnp.max(t[hh], axis=1, keepdims=True) + cq[hh]) for hh in two]
            p = [jnp.exp2(t[hh] - (m_new[hh] - cq[hh])) for hh in two]
        else:
            blocks = range(nblk)
            base = [[slope[hh] * ((i - (j0 + b)) * T).astype(F32) for b in blocks] for hh in two]
            t = [[s[hh][:, b * T:(b + 1) * T] - sd0[hh] for b in blocks] for hh in two]
            if diagonal:
                t = [[jnp.where(causal, t[hh][b], NEG) for b in blocks] for hh in two]
            cand = [[jnp.max(t[hh][b], axis=1, keepdims=True) - base[hh][b] for b in blocks] for hh in two]
            shift = base
            if not diagonal:
                picked = [[jnp.sum(jnp.where(blane == j0 + b, sel[hh], 0.0), axis=1, keepdims=True) > 0.5
                           for b in blocks] for hh in two]
                cand = [[jnp.where(picked[hh][b], cand[hh][b], NEG) for b in blocks] for hh in two]
                shift = [[jnp.where(picked[hh][b], base[hh][b], -NEG) for b in blocks] for hh in two]
            m_new = [functools.reduce(jnp.maximum, cand[hh], m[hh]) for hh in two]
            ps = [[jnp.exp2(t[hh][b] - (m_new[hh] + shift[hh][b])) for b in blocks] for hh in two]
            p = [ps[hh][0] if nblk == 1 else jnp.concatenate(ps[hh], axis=1) for hh in two]
        yield
        alpha = [jnp.exp2(m[hh] - m_new[hh]) for hh in two]
        l = [alpha[hh] * carry[hh][1] + jnp.sum(p[hh], axis=1, keepdims=True) for hh in two]
        yield
        pv = [jnp.dot(p[hh].astype(BF16), v2, preferred_element_type=F32) for hh in two]
        return tuple((m_new[hh], l[hh], alpha[hh] * carry[hh][2] + pv[hh]) for hh in two)

    chunk = lambda *a: _lockstep([chunk_gen(*a)])[0]
    init = tuple((jnp.full((T, 1), NEG, F32), jnp.zeros((T, 1), F32), jnp.zeros((T, LANES), F32)) for _ in range(2))
    if mode_parts:
        return chunk_gen, init, lo
    carry = chunk(i, 1, init, True)
    done = 0
    for w in WIDE:
        n = (i - done) // w
        carry = lax.fori_loop(0, n, lambda jw, cr, w=w, done=done: chunk(done + jw * w, w, cr, False), carry)
        done = done + n * w
    carry = lax.fori_loop(done, i, lambda j, cr: chunk(j, 1, cr, False), carry)
    (_, l_a, acc_a), (_, l_b, acc_b) = carry
    o_ref[...] = jnp.where(lo, acc_a / l_a, acc_b / l_b)


def _prompt_attn(mode, q, kb, vb, n_rows, *extra):
    nq = n_rows // ATT_TILE
    qspec = pl.BlockSpec((ATT_TILE, LANES), lambda hp, i, *_: (i, hp))
    kvspec = pl.BlockSpec((n_rows, LANES), lambda hp, i, *_: (0, hp))
    if mode == "fox":
        c, ct = extra
        in_specs = [qspec, kvspec, kvspec, pl.BlockSpec((ATT_TILE, N_HEADS), lambda hp, i: (i, 0)),
                    _const_spec((N_HEADS, n_rows))]
        args = (q, kb, vb, c, ct)
        n_prefetch = 0
    else:
        slopes, kmean = extra
        in_specs = [qspec, kvspec, kvspec, pl.BlockSpec((kmean.shape[0], LANES), lambda hp, i, *_: (0, hp))]
        args = (slopes, q, kb, vb, kmean)
        n_prefetch = 1
    return pl.pallas_call(
        functools.partial(_prompt_attn_kernel, mode=mode),
        grid_spec=pltpu.PrefetchScalarGridSpec(
            num_scalar_prefetch=n_prefetch, grid=(N_HEADS // 2, nq), in_specs=in_specs, out_specs=qspec),
        out_shape=jax.ShapeDtypeStruct((n_rows, WIDTH), F32),
        compiler_params=_cparams("arbitrary", "arbitrary"),
        name=mode + "_prompt",
    )(*args)


def _head_rows(x8):
    return jnp.concatenate([jnp.broadcast_to(x8[h:h + 1, :], (8, x8.shape[1])) for h in range(N_HEADS)], axis=0)


def _block_diag_q(q, scale):
    lane = lax.broadcasted_iota(jnp.int32, (1, WIDTH), 1)
    return jnp.concatenate([jnp.where(lane // HEAD_DIM == h, q * scale, 0.0) for h in range(N_HEADS)], axis=0)


def _pages_t(cache):
    d, n = cache.shape[:2]
    return jnp.transpose(cache, (0, 1, 3, 4, 2)).reshape(d, n, WIDTH, PAGE)


def _page_spec(layer, n_pages, i, reverse):
    def index(b, p, pt, *_):
        pg = p * PAGES_PER_STEP + i
        if reverse:
            pg = n_pages - 1 - pg
        return (layer, pt[b, pg], 0, 0)
    return pl.BlockSpec((None, None, WIDTH, PAGE), index)


def _sample_select_kernel(pt_ref, q_ref, knew_ref, *refs, n_pages):
    P = PAGES_PER_STEP
    k_refs, sel_ref, kmt_ref = refs[:P], refs[P], refs[P + 1]
    p = pl.program_id(1)
    per_blk = MOBA_BLOCK // PAGE
    lane = lax.broadcasted_iota(jnp.int32, (1, LANES), 1)

    @pl.when(p == 0)
    def _():
        kmt_ref[...] = jnp.zeros(kmt_ref.shape, F32)

    kmt = kmt_ref[...]
    for i in range(P // per_blk):
        tot = sum(k_refs[per_blk * i + t][...] for t in range(per_blk))
        col = jnp.sum(tot, axis=1, keepdims=True) * (1.0 / MOBA_BLOCK)
        kmt = jnp.where(lane == p * (P // per_blk) + i, col, kmt)
    kmt_ref[...] = kmt

    @pl.when(p == pl.num_programs(1) - 1)
    def _():
        n_past = n_pages // per_blk
        qbd = _block_diag_q(q_ref[...], 1.0)
        bs = _dot_hi(qbd, kmt_ref[...])
        own_mean = jnp.sum(knew_ref[...], axis=0, keepdims=True) * (1.0 / MOBA_BLOCK)
        bs = jnp.where(lane == n_past, jnp.sum(qbd * own_mean, axis=1, keepdims=True), bs)
        sel_ref[...] = _top3(bs, lane < n_past, n_past)


def _sample_select(page_table, q, knew, cache_kt, layer, row0):
    n_seq, n_pages = page_table.shape
    t = knew.shape[0] // n_seq
    P = PAGES_PER_STEP
    return pl.pallas_call(
        functools.partial(_sample_select_kernel, n_pages=n_pages),
        grid_spec=pltpu.PrefetchScalarGridSpec(
            num_scalar_prefetch=1, grid=(n_seq, n_pages // P),
            in_specs=[pl.BlockSpec((t, WIDTH), lambda b, p, pt: (row0 // t + b, 0)),
                      pl.BlockSpec((t, WIDTH), lambda b, p, pt: (b, 0))]
            + [_page_spec(layer, n_pages, i, False) for i in range(P)],
            out_specs=pl.BlockSpec((None, N_HEADS * t, LANES), lambda b, p, pt: (b, 0, 0)),
            scratch_shapes=[pltpu.VMEM((WIDTH, LANES), F32)]),
        out_shape=jax.ShapeDtypeStruct((n_seq, N_HEADS * t, LANES), F32),
        compiler_params=_cparams("arbitrary", "arbitrary"),
        name="moba_sample_select",
    )(page_table, q, knew, *([cache_kt] * P))


def _sample_attn_kernel(*refs, mode, n_pages):
    P = PAGES_PER_STEP
    if mode == "fox":
        pt_ref, q_ref, knew_ref, vnew_ref, lnew_ref = refs[:5]
        rest = refs[5:]
        lf_refs, rest = rest[:P], rest[P:]
    else:
        pt_ref, slopes_ref, q_ref, knew_ref, vnew_ref, sel_ref = refs[:6]
        rest = refs[6:]
    k_refs, v_refs, rest = rest[:P], rest[P:2 * P], rest[2 * P:]
    o_ref, qbd_ref, m_ref, l_ref, acc_ref, carry_ref, cq_ref = rest
    p = pl.program_id(1)
    R = qbd_ref.shape[0]
    t_new = R // N_HEADS
    lane = lax.broadcasted_iota(jnp.int32, (1, LANES), 1)
    trow = lax.broadcasted_iota(jnp.int32, (R, 1), 0) % t_new
    if mode == "moba":
        srow = jnp.concatenate([jnp.full((t_new, 1), slopes_ref[h], F32) for h in range(N_HEADS)], axis=0)

    @pl.when(p == 0)
    def _():
        qbd_ref[...] = _block_diag_q(q_ref[...], ATTN_SCALE).astype(BF16)
        pad = jnp.zeros((PAGE - t_new, WIDTH), F32)
        s = _dot_nt(qbd_ref[...], jnp.concatenate([knew_ref[...], pad], axis=0))
        if mode == "fox":
            x = lnew_ref[...]
            d = 1
            while d < t_new:
                x = x + jnp.where(lane >= d, pltpu.roll(x, d, 1), 0.0)
                d *= 2
            crep = _head_rows(x)
            cq = jnp.sum(jnp.where(lane == trow, crep, 0.0), axis=1, keepdims=True)
            cq_ref[...] = cq
            carry_ref[...] = jnp.zeros(carry_ref.shape, F32)
            s = s + cq - crep
        else:
            s = s - srow * (trow - lane).astype(F32)
        s = jnp.where(lane <= trow, s, NEG)
        m = jnp.max(s, axis=1, keepdims=True)
        pr = jnp.exp(s - m)
        m_ref[...] = m
        l_ref[...] = jnp.sum(pr, axis=1, keepdims=True)
        acc_ref[...] = _dot(pr, jnp.concatenate([vnew_ref[...], pad], axis=0))

    qbd = qbd_ref[...]
    parts = []
    if mode == "fox":
        carry = carry_ref[...]
    for i in range(P):
        pg = n_pages - 1 - (p * P + i)
        s = _dot(qbd, k_refs[i][...])
        if mode == "fox":
            lf = lf_refs[i][...]
            x = lf
            d = 1
            while d < PAGE:
                x = x + jnp.where(lane < PAGE - d, pltpu.roll(x, PAGE - d, 1), 0.0)
                d *= 2
            bias = (x - lf) + carry
            carry = carry + x[:, 0:1]
            s = s + _head_rows(bias) + cq_ref[...]
        else:
            dist = (n_pages * PAGE + trow) - (pg * PAGE + lane)
            s = s - srow * dist.astype(F32)
            picked = jnp.sum(jnp.where(lane == pg // (MOBA_BLOCK // PAGE), sel_ref[...], 0.0), axis=1, keepdims=True) > 0.5
            s = jnp.where(picked, s, NEG)
        parts.append(s)
    if mode == "fox":
        carry_ref[...] = carry
    s_all = jnp.concatenate(parts, axis=1)
    m_old = m_ref[...]
    m_new = jnp.maximum(m_old, jnp.max(s_all, axis=1, keepdims=True))
    alpha = jnp.exp(m_old - m_new)
    pr = jnp.exp(s_all - m_new)
    m_ref[...] = m_new
    l_ref[...] = alpha * l_ref[...] + jnp.sum(pr, axis=1, keepdims=True)
    pv = sum(_dot_nt(pr[:, i * PAGE:(i + 1) * PAGE], v_refs[i][...]) for i in range(P))
    acc_ref[...] = alpha * acc_ref[...] + pv

    @pl.when(p == pl.num_programs(1) - 1)
    def _():
        res = acc_ref[...] / l_ref[...]
        wl = lax.broadcasted_iota(jnp.int32, (1, WIDTH), 1)
        out = jnp.zeros((t_new, WIDTH), F32)
        for h in range(N_HEADS):
            out = jnp.where(wl // HEAD_DIM == h, res[h * t_new:(h + 1) * t_new, :], out)
        o_ref[...] = out


def _sample_attn(mode, page_table, q, knew, vnew, cache_kt, cache_vt, layer, row0, *extra):
    n_seq, n_pages = page_table.shape
    t = knew.shape[0] // n_seq
    R = N_HEADS * t
    P = PAGES_PER_STEP
    nsp = 1 if mode == "fox" else 2
    qtok = pl.BlockSpec((t, WIDTH), lambda b, p, *_: (row0 // t + b, 0))
    tok = pl.BlockSpec((t, WIDTH), lambda b, p, *_: (b, 0))
    pages = [_page_spec(layer, n_pages, i, True) for i in range(P)]
    if mode == "fox":
        lnew_t, lf_t = extra
        def lf_spec(i):
            return pl.BlockSpec((None, N_HEADS, PAGE), lambda b, p, pt: (pt[b, n_pages - 1 - (p * P + i)], 0, 0))
        in_specs = ([qtok, tok, tok, pl.BlockSpec((None, N_HEADS, LANES), lambda b, p, pt: (b, 0, 0))]
                    + [lf_spec(i) for i in range(P)] + pages + pages)
        args = (page_table, q, knew, vnew, lnew_t, *([lf_t] * P), *([cache_kt] * P), *([cache_vt] * P))
    else:
        slopes, sel = extra
        in_specs = [qtok, tok, tok, pl.BlockSpec((None, R, LANES), lambda b, p, pt, sl: (b, 0, 0))] + pages + pages
        args = (page_table, slopes, q, knew, vnew, sel, *([cache_kt] * P), *([cache_vt] * P))
    return pl.pallas_call(
        functools.partial(_sample_attn_kernel, mode=mode, n_pages=n_pages),
        grid_spec=pltpu.PrefetchScalarGridSpec(
            num_scalar_prefetch=nsp, grid=(n_seq, n_pages // P), in_specs=in_specs, out_specs=tok,
            scratch_shapes=[pltpu.VMEM((R, WIDTH), BF16), pltpu.VMEM((R, 1), F32), pltpu.VMEM((R, 1), F32),
                            pltpu.VMEM((R, WIDTH), F32), pltpu.VMEM((N_HEADS, 1), F32), pltpu.VMEM((R, 1), F32)]),
        out_shape=jax.ShapeDtypeStruct((n_seq * t, WIDTH), F32),
        compiler_params=_cparams("arbitrary", "arbitrary"),
        name=mode + "_sample",
    )(*args)


def _merge_kernel(h_ref, gate_ref, orw_ref, ofox_ref, omoba_ref, wr_ref, wf_ref, wm_ref, wo_ref, g_ref, o_ref):
    m = (gate_ref[:, 0:D_MODEL] * _dot(orw_ref[...], wr_ref[...])
         + gate_ref[:, D_MODEL:2 * D_MODEL] * _dot(ofox_ref[...], wf_ref[...])
         + gate_ref[:, 2 * D_MODEL:3 * D_MODEL] * _dot(omoba_ref[...], wm_ref[...]))
    o_ref[...] = h_ref[...] + _rms(_dot(m, wo_ref[...]), g_ref[...])


def _merge(h, gates, o_rw, o_fox, o_moba, wr, wf, wm, wo, g):
    m = h.shape[0]

    def row(n):
        return pl.BlockSpec((ROW_TILE, n), lambda i: (i, 0))

    wspec = _const_spec((WIDTH, D_MODEL))
    return pl.pallas_call(
        _merge_kernel,
        grid=(m // ROW_TILE,),
        in_specs=[row(D_MODEL), row(3 * D_MODEL), row(WIDTH), row(WIDTH), row(WIDTH), wspec, wspec, wspec,
                  _const_spec((D_MODEL, D_MODEL)), _const_spec((1, D_MODEL))],
        out_specs=row(D_MODEL),
        out_shape=jax.ShapeDtypeStruct((m, D_MODEL), F32),
        compiler_params=_cparams("arbitrary"),
        name="merge",
    )(h, gates, o_rw, o_fox, o_moba, wr, wf, wm, wo, g)


def _pack_w_in(w_in, b_forget):
    o = 0
    def take(n):
        nonlocal o
        s = w_in[:, o:o + n]
        o += n
        return s
    rkv = take(3 * WIDTH)
    wd, ad, gd = take(W_LORA), take(A_LORA), take(G_LORA)
    fox = take(3 * WIDTH)
    fl = take(N_HEADS)
    moba = take(3 * WIDTH)
    gates = take(3 * D_MODEL)
    z = lambda n: jnp.zeros((D_MODEL, n), w_in.dtype)
    packed = jnp.concatenate([rkv, wd, z(LANES - W_LORA), ad, z(LANES - A_LORA), gd, fox, moba, gates,
                              fl, z(LANES - N_HEADS)], axis=1)
    bf = jnp.concatenate([b_forget, jnp.zeros((LANES - N_HEADS,), F32)])[None, :]
    return packed.astype(BF16), bf


def _pad_rows(w, n):
    return jnp.concatenate([w, jnp.zeros((n - w.shape[0], w.shape[1]), w.dtype)], axis=0)


def _pack_mu(mu):
    z = jnp.zeros((LANES - W_LORA,), F32)
    return jnp.concatenate([mu[:3 * WIDTH], mu[3 * WIDTH:3 * WIDTH + W_LORA], z,
                            mu[3 * WIDTH + W_LORA:3 * WIDTH + W_LORA + A_LORA], z,
                            mu[3 * WIDTH + W_LORA + A_LORA:]])[None, :]


def _block_diag_state(s):
    b = s.shape[0]
    eye = jnp.eye(N_HEADS, dtype=s.dtype)
    return jnp.einsum('bhij,hg->bhigj', s, eye).reshape(b, WIDTH, WIDTH)


def _unblock_state(s):
    b = s.shape[0]
    s5 = s.reshape(b, N_HEADS, HEAD_DIM, N_HEADS, HEAD_DIM)
    return jnp.stack([s5[:, h, :, h, :] for h in range(N_HEADS)], axis=1)


def kernel(x_prompt, x_sample, cache_fox_k, cache_fox_v, cache_fox_logf, cache_moba_k, cache_moba_v, state_rwkv_wkv, state_rwkv_shift, page_table, norms, w_in, b_forget, rwkv_mu, rwkv_w0, rwkv_w_up, rwkv_a0, rwkv_a_up, rwkv_g_up, rwkv_k_k, rwkv_k_a, rwkv_r_k, rwkv_ln_w, rwkv_ln_b, w_proj_rwkv, w_proj_fox, w_proj_moba, w_out, ffn_gate, ffn_up, ffn_down):
    depth = norms.shape[0]
    n_p = x_prompt.shape[0] * x_prompt.shape[1]
    n_seq, t_s = x_sample.shape[0], x_sample.shape[1]
    n_s = n_seq * t_s
    assert x_prompt.shape[0] == 1 and n_p % ROW_TILE == 0 and n_s % ROW_TILE == 0 and n_p % (8 * MOBA_BLOCK) == 0

    x = jnp.concatenate([x_prompt.reshape(n_p, D_MODEL), x_sample.reshape(n_s, D_MODEL)], axis=0)
    slopes = 2.0 ** (-8.0 * jnp.arange(1, N_HEADS + 1, dtype=F32) / N_HEADS)
    hid = jnp.arange(WIDTH) // HEAD_DIM
    bd = (hid[:, None] == hid[None, :]).astype(BF16)
    ar = jnp.arange(LANES)
    tri = (ar[:, None] <= ar[None, :]).astype(BF16)
    n_rows_c = N_HEADS * n_p // LANES
    rr = jnp.arange(n_rows_c)
    blk = ((rr[:, None] // (n_p // LANES) == rr[None, :] // (n_p // LANES)) & (rr[None, :] < rr[:, None])).astype(BF16)
    cfk, cfv, cmk, cmv = (_pages_t(c) for c in (cache_fox_k, cache_fox_v, cache_moba_k, cache_moba_v))
    vec = lambda a: a.reshape(1, -1)

    outs_p, outs_s = [], []
    for l in range(depth):
        bfc = lambda a: a.astype(BF16)
        h = _ffn(x, vec(norms[l, 0]), vec(norms[l, 1]), bfc(ffn_gate[l, 0]), bfc(ffn_up[l, 0]), bfc(ffn_down[l, 0]))
        w_pack, bf_pad = _pack_w_in(w_in[l], b_forget[l])
        (u, rw, fq, fk, fv, fkb, fvb, logf, mq, mk, mv, mkb, mvb, gates) = _inproj(h, vec(norms[l, 2]), w_pack, bf_pad)

        prev_u = jnp.concatenate([jnp.zeros((8, D_MODEL), F32), state_rwkv_shift[l]], axis=0)
        prev_rw = _mm(prev_u, w_pack[:, :RW_PACK])
        rparams = (_pack_mu(rwkv_mu[l]), vec(rwkv_w0[l]), bfc(_pad_rows(rwkv_w_up[l], LANES)), vec(rwkv_a0[l]),
                   bfc(_pad_rows(rwkv_a_up[l], LANES)), bfc(rwkv_g_up[l]), vec(rwkv_k_k[l]), vec(rwkv_k_a[l]),
                   vec(rwkv_r_k[l]), vec(rwkv_ln_w[l]), vec(rwkv_ln_b[l]))
        o_rw_p, s_p = _rwkv(rw, prev_rw[0:1][None], jnp.zeros((1, WIDTH, WIDTH), F32), rparams, bd,
                            n_seq=1, t_seq=n_p, row0=0)
        o_rw_s, s_s = _rwkv(rw, prev_rw[8:8 + n_seq][:, None, :], _block_diag_state(state_rwkv_wkv[l]), rparams, bd,
                            n_seq=n_seq, t_seq=t_s, row0=n_p)

        lf_t = logf[:n_p, :N_HEADS].T
        ct = _cumsum_rows(lf_t.reshape(n_rows_c, LANES), tri, blk).reshape(N_HEADS, n_p)
        kmean = _kmean(mk, n_p)
        o_fox_p, o_moba_p = _both_prompt(fq, fkb, fvb, ct.T, ct, mq, mkb, mvb, slopes, kmean, n_p)
        lnew_t = jnp.swapaxes(logf[n_p:, :N_HEADS].reshape(n_seq, t_s, N_HEADS), 1, 2)
        lnew_t = jnp.concatenate([lnew_t, jnp.zeros((n_seq, N_HEADS, LANES - t_s), F32)], axis=2)
        lf_pool_t = jnp.swapaxes(cache_fox_logf[l].astype(F32), 1, 2)
        o_fox_s = _sample_attn("fox", page_table, fq, fk[n_p:], fv[n_p:], cfk, cfv, l, n_p, lnew_t, lf_pool_t)

        sel = _sample_select(page_table, mq, mk[n_p:], cmk, l, n_p)
        o_moba_s = _sample_attn("moba", page_table, mq, mk[n_p:], mv[n_p:], cmk, cmv, l, n_p, slopes, sel)

        cat = lambda a, b: jnp.concatenate([a, b], axis=0)
        h = _merge(h, gates, cat(o_rw_p, o_rw_s), cat(o_fox_p, o_fox_s), cat(o_moba_p, o_moba_s),
                   bfc(w_proj_rwkv[l]), bfc(w_proj_fox[l]), bfc(w_proj_moba[l]), bfc(w_out[l]), vec(norms[l, 3]))
        x = _ffn(h, vec(norms[l, 4]), vec(norms[l, 5]), bfc(ffn_gate[l, 1]), bfc(ffn_up[l, 1]), bfc(ffn_down[l, 1]))

        hd = lambda a, b, t: a.reshape(b, t, N_HEADS, HEAD_DIM)
        outs_p.append((hd(fk[:n_p], 1, n_p), hd(fv[:n_p], 1, n_p), logf[:n_p, :N_HEADS].reshape(1, n_p, N_HEADS),
                       hd(mk[:n_p], 1, n_p), hd(mv[:n_p], 1, n_p), _unblock_state(s_p), u[n_p - 1:n_p]))
        outs_s.append((hd(fk[n_p:], n_seq, t_s), hd(fv[n_p:], n_seq, t_s),
                       logf[n_p:, :N_HEADS].reshape(n_seq, t_s, N_HEADS), hd(mk[n_p:], n_seq, t_s),
                       hd(mv[n_p:], n_seq, t_s), _unblock_state(s_s),
                       u[n_p:].reshape(n_seq, t_s, D_MODEL)[:, -1]))

    st = lambda rows, i: jnp.stack([r[i] for r in rows])
    return (x[:n_p].reshape(x_prompt.shape), x[n_p:].reshape(x_sample.shape),
            *[st(outs_p, i) for i in range(7)], *[st(outs_s, i) for i in range(7)])
```
